```python
import math, functools
import jax, jax.numpy as jnp
from jax import lax
import numpy as np

D_MODEL = 1024
BATCH = 2
SEQ = 8192
DEPTH = 1
DEC_BATCH = 128
DEC_SEQ = 4
PAST_LEN = 8192
PAGE_SIZE = 128

MLA_HEADS = 8
MLA_NOPE = 64
MLA_ROPE = 32
MLA_VDIM = 64
Q_LORA = 256
KV_LORA = 256
MLA_WIDTH = MLA_HEADS * MLA_VDIM
MLA_SCALE = 1.0 / math.sqrt(MLA_NOPE + MLA_ROPE)
ROPE_BASE = 10000.0
SB_HEADS = 8
SB_HEAD_DIM = 64
SB_WIDTH = SB_HEADS * SB_HEAD_DIM
SB_SCALE = 1.0 / math.sqrt(SB_HEAD_DIM)
PLE_DIM = 256
Q_BLOCK = 128
EPS = 1e-6
NEG_INF = -1e30
IN_SIZES = (Q_LORA, KV_LORA, MLA_ROPE, MLA_WIDTH, SB_WIDTH, SB_WIDTH, SB_WIDTH, SB_WIDTH, D_MODEL, D_MODEL)
IN_COLS = Q_LORA + KV_LORA + MLA_ROPE + MLA_WIDTH + 4 * SB_WIDTH + 2 * D_MODEL

kernel_name = "mla_stickbreaking_parallel_decoder_step"


def _rmsnorm(x, g):
    xf = x.astype(jnp.float32)
    y = xf * lax.rsqrt(jnp.mean(xf * xf, axis=-1, keepdims=True) + EPS)
    return (y * g.astype(jnp.float32)).astype(x.dtype)


def _rope(x, pos):
    dr = x.shape[-1]
    freqs = ROPE_BASE ** (-jnp.arange(0, dr, 2, dtype=jnp.float32) / dr)
    ang = pos.astype(jnp.float32)[:, None] * freqs[None, :]
    shape = (1, ang.shape[0]) + (1,) * (x.ndim - 3) + (dr // 2,)
    c = jnp.cos(ang).reshape(shape).astype(x.dtype)
    s = jnp.sin(ang).reshape(shape).astype(x.dtype)
    x1, x2 = x[..., : dr // 2], x[..., dr // 2:]
    return jnp.concatenate([x1 * c - x2 * s, x1 * s + x2 * c], axis=-1)


def _project_inputs(h, pos, pre_g, w_in, q_norm_g, w_uq, kv_norm_g, w_uk):
    b, t = h.shape[0], h.shape[1]
    xn = _rmsnorm(h, pre_g)
    z = xn @ w_in
    offsets = [int(v) for v in np.cumsum(IN_SIZES)[:-1]]
    cq, ckv, kr, gate_a, sq, sk, sv, gate_b, ma, mb = jnp.split(z, offsets, axis=-1)
    q = (_rmsnorm(cq, q_norm_g) @ w_uq).reshape(b, t, MLA_HEADS, MLA_NOPE + MLA_ROPE)
    q_nope, q_rope = q[..., :MLA_NOPE], _rope(q[..., MLA_NOPE:], pos)
    q_lat = jnp.einsum('bthn,rhn->bthr', q_nope, w_uk)
    ckv = _rmsnorm(ckv, kv_norm_g)
    krope = _rope(kr, pos)
    sq = sq.reshape(b, t, SB_HEADS, SB_HEAD_DIM)
    sk = sk.reshape(b, t, SB_HEADS, SB_HEAD_DIM)
    sv = sv.reshape(b, t, SB_HEADS, SB_HEAD_DIM)
    return q_lat, q_rope, ckv, krope, sq, sk, sv, gate_a, gate_b, ma, mb


def _mla_attend(q_lat, q_rope, q_pos, ckv, krope, k_pos):
    s = (jnp.einsum('bqhr,bkr->bhqk', q_lat, ckv)
         + jnp.einsum('bqhd,bkd->bhqk', q_rope, krope)).astype(jnp.float32) * MLA_SCALE
    s = jnp.where(k_pos[None, :] <= q_pos[:, None], s, NEG_INF)
    p = jax.nn.softmax(s, axis=-1).astype(ckv.dtype)
    return jnp.einsum('bhqk,bkr->bqhr', p, ckv)


def _sb_attend(q, q_pos, k, v, k_pos):
    z = jnp.einsum('bqhd,bkhd->bhqk', q, k).astype(jnp.float32) * SB_SCALE
    mask = k_pos[None, :] < q_pos[:, None]
    log_beta = jax.nn.log_sigmoid(z)
    log_1m = jnp.where(mask, jax.nn.log_sigmoid(-z), 0.0)
    between = lax.cumsum(log_1m, axis=3, reverse=True) - log_1m
    a = jnp.where(mask, jnp.exp(log_beta + between), 0.0).astype(v.dtype)
    return jnp.einsum('bhqk,bkhd->bqhd', a, v)


def _sweep_query_blocks(attend, q_arrays, q_pos):
    t = q_pos.shape[0]
    nb = t // Q_BLOCK

    def split(a):
        return jnp.moveaxis(a.reshape((a.shape[0], nb, Q_BLOCK) + a.shape[2:]), 1, 0)

    xs = (tuple(split(a) for a in q_arrays), q_pos.reshape(nb, Q_BLOCK))
    out = lax.map(lambda args: attend(*args[0], args[1]), xs)
    out = jnp.moveaxis(out, 0, 1)
    return out.reshape((out.shape[0], t) + out.shape[3:])


def _gather_pages(pool, page_table):
    g = pool[page_table]
    return g.reshape((g.shape[0], g.shape[1] * g.shape[2]) + g.shape[3:])


def _merge_outputs(h, o_lat, o_sb, gate_a, gate_b, ma, mb, p_emb, w_uv, w_branch_a, w_branch_b,
                   w_out, post_g, ple_g, w_ple_gate, w_ple):
    b, t = h.shape[0], h.shape[1]
    va = jnp.einsum('bthr,rhd->bthd', o_lat, w_uv).reshape(b, t, MLA_WIDTH) * jax.nn.silu(gate_a)
    vb = o_sb.reshape(b, t, SB_WIDTH) * jax.nn.silu(gate_b)
    ya = va @ w_branch_a
    yb = vb @ w_branch_b
    mixed = (jax.nn.sigmoid(ma) * ya + jax.nn.sigmoid(mb) * yb) @ w_out
    h = h + _rmsnorm(mixed, post_g)
    h = h + jax.nn.sigmoid(_rmsnorm(h, ple_g) @ w_ple_gate) * (p_emb @ w_ple)
    return h


def setup_inputs(seed: int = 0) -> dict:
    key = jax.random.key(seed)
    ks = jax.random.split(key, 32)
    n_pages = PAST_LEN // PAGE_SIZE
    n_used = DEC_BATCH * n_pages
    n_phys = (n_used * 5) // 4
    f32 = jnp.float32

    def nrm(k, shape, scale=1.0):
        return jax.random.normal(k, shape, f32) * scale

    def gain(k, shape):
        return 1.0 + 0.02 * jax.random.normal(k, shape, f32)

    perm = jax.random.permutation(ks[0], n_phys)
    page_table = perm[:n_used].reshape(DEC_BATCH, n_pages).astype(jnp.int32)
    return {
        "x_prompt": nrm(ks[1], (BATCH, SEQ, D_MODEL)),
        "x_sample": nrm(ks[2], (DEC_BATCH, DEC_SEQ, D_MODEL)),
        "cache_mla_ckv": nrm(ks[3], (DEPTH, n_phys, PAGE_SIZE, KV_LORA)),
        "cache_mla_krope": nrm(ks[4], (DEPTH, n_phys, PAGE_SIZE, MLA_ROPE)),
        "cache_sb_k": nrm(ks[5], (DEPTH, n_phys, PAGE_SIZE, SB_HEADS, SB_HEAD_DIM)),
        "cache_sb_v": nrm(ks[6], (DEPTH, n_phys, PAGE_SIZE, SB_HEADS, SB_HEAD_DIM)),
        "page_table": page_table,
        "p_prompt": nrm(ks[7], (DEPTH, BATCH, SEQ, PLE_DIM)),
        "p_sample": nrm(ks[8], (DEPTH, DEC_BATCH, DEC_SEQ, PLE_DIM)),
        "pre_norm_g": gain(ks[9], (DEPTH, D_MODEL)),
        "w_in": nrm(ks[10], (DEPTH, D_MODEL, IN_COLS), D_MODEL ** -0.5),
        "q_norm_g": gain(ks[11], (DEPTH, Q_LORA)),
        "w_uq": nrm(ks[12], (DEPTH, Q_LORA, MLA_HEADS * (MLA_NOPE + MLA_ROPE)), Q_LORA ** -0.5),
        "kv_norm_g": gain(ks[13], (DEPTH, KV_LORA)),
        "w_uk": nrm(ks[14], (DEPTH, KV_LORA, MLA_HEADS, MLA_NOPE), KV_LORA ** -0.5),
        "w_uv": nrm(ks[15], (DEPTH, KV_LORA, MLA_HEADS, MLA_VDIM), KV_LORA ** -0.5),
        "w_branch_a": nrm(ks[16], (DEPTH, MLA_WIDTH, D_MODEL), MLA_WIDTH ** -0.5),
        "w_branch_b": nrm(ks[17], (DEPTH, SB_WIDTH, D_MODEL), SB_WIDTH ** -0.5),
        "w_out": nrm(ks[18], (DEPTH, D_MODEL, D_MODEL), D_MODEL ** -0.5),
        "post_norm_g": gain(ks[19], (DEPTH, D_MODEL)),
        "ple_norm_g": gain(ks[20], (DEPTH, D_MODEL)),
        "w_ple_gate": nrm(ks[21], (DEPTH, D_MODEL, D_MODEL), D_MODEL ** -0.5),
        "w_ple": nrm(ks[22], (DEPTH, PLE_DIM, D_MODEL), PLE_DIM ** -0.5),
    }


def reference(x_prompt, x_sample, cache_mla_ckv, cache_mla_krope, cache_sb_k, cache_sb_v, page_table,
              p_prompt, p_sample, pre_norm_g, w_in, q_norm_g, w_uq, kv_norm_g, w_uk, w_uv,
              w_branch_a, w_branch_b, w_out, post_norm_g, ple_norm_g, w_ple_gate, w_ple):
    pos_p = jnp.arange(SEQ, dtype=jnp.int32)
    pos_s = PAST_LEN + jnp.arange(DEC_SEQ, dtype=jnp.int32)
    k_pos_s = jnp.arange(PAST_LEN + DEC_SEQ, dtype=jnp.int32)
    hp, hs = x_prompt, x_sample
    ckv_p, krope_p, sbk_p, sbv_p = [], [], [], []
    ckv_s, krope_s, sbk_s, sbv_s = [], [], [], []
    for i in range(DEPTH):
        proj = functools.partial(_project_inputs, pre_g=pre_norm_g[i], w_in=w_in[i], q_norm_g=q_norm_g[i],
                                 w_uq=w_uq[i], kv_norm_g=kv_norm_g[i], w_uk=w_uk[i])
        merge = functools.partial(_merge_outputs, w_uv=w_uv[i], w_branch_a=w_branch_a[i],
                                  w_branch_b=w_branch_b[i], w_out=w_out[i], post_g=post_norm_g[i],
                                  ple_g=ple_norm_g[i], w_ple_gate=w_ple_gate[i], w_ple=w_ple[i])
        q_lat, q_rope, ckv, krope, sq, sk, sv, ga, gb, ma, mb = proj(hp, pos_p)
        o_lat = _sweep_query_blocks(functools.partial(_mla_attend, ckv=ckv, krope=krope, k_pos=pos_p),
                                    (q_lat, q_rope), pos_p)
        o_sb = _sweep_query_blocks(functools.partial(_sb_attend, k=sk, v=sv, k_pos=pos_p), (sq,), pos_p)
        hp = merge(hp, o_lat, o_sb, ga, gb, ma, mb, p_prompt[i])
        ckv_p.append(ckv); krope_p.append(krope); sbk_p.append(sk); sbv_p.append(sv)
        q_lat, q_rope, ckv, krope, sq, sk, sv, ga, gb, ma, mb = proj(hs, pos_s)
        ckv_all = jnp.concatenate([_gather_pages(cache_mla_ckv[i], page_table), ckv], axis=1)
        krope_all = jnp.concatenate([_gather_pages(cache_mla_krope[i], page_table), krope], axis=1)
        k_all = jnp.concatenate([_gather_pages(cache_sb_k[i], page_table), sk], axis=1)
        v_all = jnp.concatenate([_gather_pages(cache_sb_v[i], page_table), sv], axis=1)
        o_lat = _mla_attend(q_lat, q_rope, pos_s, ckv_all, krope_all, k_pos_s)
        o_sb = _sb_attend(sq, pos_s, k_all, v_all, k_pos_s)
        hs = merge(hs, o_lat, o_sb, ga, gb, ma, mb, p_sample[i])
        ckv_s.append(ckv); krope_s.append(krope); sbk_s.append(sk); sbv_s.append(sv)
    y_prompt, y_sample = hp, hs
    new_ckv_prompt = jnp.stack(ckv_p, axis=0)
    new_krope_prompt = jnp.stack(krope_p, axis=0)
    new_sbk_prompt = jnp.stack(sbk_p, axis=0)
    new_sbv_prompt = jnp.stack(sbv_p, axis=0)
    new_ckv_sample = jnp.stack(ckv_s, axis=0)
    new_krope_sample = jnp.stack(krope_s, axis=0)
    new_sbk_sample = jnp.stack(sbk_s, axis=0)
    new_sbv_sample = jnp.stack(sbv_s, axis=0)
    return (y_prompt, y_sample, new_ckv_prompt, new_krope_prompt, new_sbk_prompt, new_sbv_prompt,
            new_ckv_sample, new_krope_sample, new_sbk_sample, new_sbv_sample)
```

```python
import functools
import math

import jax
import jax.numpy as jnp
from jax import lax
from jax.experimental import pallas as pl
from jax.experimental.pallas import tpu as pltpu

F32 = jnp.float32
BF16 = jnp.bfloat16
EPS = 1e-6
ROPE_BASE = 10000.0
NEG_INF = -1e30
LANES = 128
MXU_TILE = 256
VMEM_LIMIT = 56 * 1024 * 1024

_NT = (((1,), (1,)), ((), ()))


def _cparams(sems):
    return pltpu.CompilerParams(dimension_semantics=sems, vmem_limit_bytes=VMEM_LIMIT)


def _rms(x, g):
    return x * lax.rsqrt(jnp.mean(x * x, axis=-1, keepdims=True) + EPS) * g


def _sigmoid(x):
    return jax.nn.sigmoid(x)


def _const_spec(shape):
    nd = len(shape)
    return pl.BlockSpec(shape, lambda *_: (0,) * nd)


def _fold_kernel(a_ref, b_ref, o_ref, *, scale):
    w = lax.dot_general(a_ref[0], b_ref[0], _NT, precision=lax.Precision.HIGHEST,
                        preferred_element_type=F32)
    o_ref[...] = (w * scale).astype(o_ref.dtype)


def _fold_qlat(nope_t, uk_t, scale):
    h, ql, n = nope_t.shape
    r = uk_t.shape[1]
    return pl.pallas_call(
        functools.partial(_fold_kernel, scale=scale),
        grid=(h,),
        in_specs=[pl.BlockSpec((1, ql, n), lambda i: (i, 0, 0)),
                  pl.BlockSpec((1, r, n), lambda i: (i, 0, 0))],
        out_specs=pl.BlockSpec((ql, r), lambda i: (0, i)),
        out_shape=jax.ShapeDtypeStruct((ql, h * r), BF16),
        compiler_params=_cparams(("arbitrary",)),
        name="fold_qlat",
    )(nope_t, uk_t)


def _proj_kernel(x_ref, cos_ref, sin_ref, preg_ref, qg_ref, kvg_ref, win_ref, wql_ref, wqr_ref, wqrs_ref,
                 q_ref, kc_ref, ckv_ref, kr_ref, sga_ref, sqb_ref, skb_ref, svb_ref, sk_ref, sv_ref,
                 sgb_ref, sma_ref, smb_ref, *, cols, heads, kv_lora, rope):
    xn = _rms(x_ref[...], preg_ref[...]).astype(BF16)

    def proj(name):
        lo, hi = cols[name]
        return jnp.dot(xn, win_ref[:, lo:hi], preferred_element_type=F32)

    cos = cos_ref[...]
    sin = sin_ref[...]
    cqn = _rms(proj("cq"), qg_ref[...]).astype(BF16)
    ckvn = _rms(proj("ckv"), kvg_ref[...])
    krope = proj("kra") * cos + proj("krb") * sin
    ckv_ref[...] = ckvn
    kr_ref[...] = krope[:, :rope]
    kc_ref[:, 0:kv_lora] = ckvn.astype(BF16)
    kc_ref[:, kv_lora:kv_lora + LANES] = krope.astype(BF16)

    qlat = jnp.dot(cqn, wql_ref[...], preferred_element_type=F32)
    qr = jnp.dot(cqn, wqr_ref[...], preferred_element_type=F32)
    qrs = jnp.dot(cqn, wqrs_ref[...], preferred_element_type=F32)
    for h in range(heads):
        q_ref[h, :, 0:kv_lora] = qlat[:, h * kv_lora:(h + 1) * kv_lora].astype(BF16)
        sl = slice(h * LANES, (h + 1) * LANES)
        q_ref[h, :, kv_lora:kv_lora + LANES] = (qr[:, sl] * cos + qrs[:, sl] * sin).astype(BF16)

    ga = proj("ga")
    sga_ref[...] = (ga * _sigmoid(ga)).astype(BF16)
    sqb_ref[...] = proj("sq").astype(BF16)
    sk = proj("sk")
    sk_ref[...] = sk
    skb_ref[...] = sk.astype(BF16)
    sv = proj("sv")
    sv_ref[...] = sv
    svb_ref[...] = sv.astype(BF16)
    gb = proj("gb")
    sgb_ref[...] = (gb * _sigmoid(gb)).astype(BF16)
    sma_ref[...] = _sigmoid(proj("ma")).astype(BF16)
    smb_ref[...] = _sigmoid(proj("mb")).astype(BF16)


def _proj(x2d, cos_t, sin_t, pre_g, q_g, kv_g, w_in2, w_qlat, w_qr, w_qrs, *, cols, dims, tm):
    rows, d = x2d.shape
    heads, kv_lora, rope = dims["heads"], dims["kv_lora"], dims["rope"]
    a_w, b_w = dims["mla_width"], dims["sb_width"]
    nblk = rows // tm
    ntab = cos_t.shape[0] // tm
    qk = kv_lora + LANES

    def row(n):
        return pl.BlockSpec((tm, n), lambda i: (i, 0))

    tab = pl.BlockSpec((tm, LANES), lambda i: (i % ntab, 0))
    in_specs = [row(d), tab, tab, _const_spec(pre_g.shape), _const_spec(q_g.shape), _const_spec(kv_g.shape),
                _const_spec(w_in2.shape), _const_spec(w_qlat.shape), _const_spec(w_qr.shape),
                _const_spec(w_qrs.shape)]
    out_shape = [
        jax.ShapeDtypeStruct((heads, rows, qk), BF16),
        jax.ShapeDtypeStruct((rows, qk), BF16),
        jax.ShapeDtypeStruct((rows, kv_lora), F32),
        jax.ShapeDtypeStruct((rows, rope), F32),
        jax.ShapeDtypeStruct((rows, a_w), BF16),
        jax.ShapeDtypeStruct((rows, b_w), BF16),
        jax.ShapeDtypeStruct((rows, b_w), BF16),
        jax.ShapeDtypeStruct((rows, b_w), BF16),
        jax.ShapeDtypeStruct((rows, b_w), F32),
        jax.ShapeDtypeStruct((rows, b_w), F32),
        jax.ShapeDtypeStruct((rows, b_w), BF16),
        jax.ShapeDtypeStruct((rows, d), BF16),
        jax.ShapeDtypeStruct((rows, d), BF16),
    ]
    out_specs = [pl.BlockSpec((heads, tm, qk), lambda i: (0, i, 0)), row(qk), row(kv_lora), row(rope),
                 row(a_w), row(b_w), row(b_w), row(b_w), row(b_w), row(b_w), row(b_w), row(d), row(d)]
    return pl.pallas_call(
        functools.partial(_proj_kernel, cols=cols, heads=heads, kv_lora=kv_lora, rope=rope),
        grid=(nblk,),
        in_specs=in_specs,
        out_specs=out_specs,
        out_shape=out_shape,
        compiler_params=_cparams(("parallel",)),
        name="proj",
    )(x2d, cos_t, sin_t, pre_g, q_g, kv_g, w_in2, w_qlat, w_qr, w_qrs)


def _mla_kernel(q_ref, kc_ref, wuv_ref, va_ref, m_ref, l_ref, acc_ref, *, heads, kv_lora, tq, tk):
    i = pl.program_id(1)
    rows = heads * tq
    q = q_ref[...].reshape(rows, q_ref.shape[-1])
    m_ref[...] = jnp.full(m_ref.shape, NEG_INF, F32)
    l_ref[...] = jnp.zeros(l_ref.shape, F32)
    acc_ref[...] = jnp.zeros(acc_ref.shape, F32)
    nfull = (i * tq) // tk

    def step(j, masked):
        k = kc_ref[pl.ds(pl.multiple_of(j * tk, tk), tk), :]
        s = lax.dot_general(q, k, _NT, preferred_element_type=F32)
        if masked:
            qpos = i * tq + (lax.broadcasted_iota(jnp.int32, s.shape, 0) & (tq - 1))
            kpos = j * tk + lax.broadcasted_iota(jnp.int32, s.shape, 1)
            s = jnp.where(kpos <= qpos, s, NEG_INF)
        m_prev = m_ref[...]
        m_new = jnp.maximum(m_prev, jnp.max(s, axis=1, keepdims=True))
        alpha = jnp.exp(m_prev - m_new)
        p = jnp.concatenate(
            [jnp.exp(s[:, c * LANES:(c + 1) * LANES] - m_new) for c in range(tk // LANES)], axis=1)
        l_ref[...] = alpha * l_ref[...] + jnp.sum(p, axis=1, keepdims=True)
        m_ref[...] = m_new
        pv = jnp.dot(p.astype(BF16), k[:, 0:kv_lora], preferred_element_type=F32)
        for c in range(kv_lora // LANES):
            sl = slice(c * LANES, (c + 1) * LANES)
            acc_ref[:, sl] = alpha * acc_ref[:, sl] + pv[:, sl]

    def body(j, carry):
        step(j, False)
        return carry

    lax.fori_loop(0, nfull, body, 0)
    step(nfull, True)

    inv = 1.0 / l_ref[...]
    out = jnp.zeros(va_ref.shape, F32)
    for h in range(heads):
        rs = slice(h * tq, (h + 1) * tq)
        o = jnp.concatenate([acc_ref[rs, c * LANES:(c + 1) * LANES] * inv[rs]
                             for c in range(kv_lora // LANES)], axis=1)
        out = out + jnp.dot(o.astype(BF16), wuv_ref[h], preferred_element_type=F32)
    va_ref[...] = out


def _mla_prompt(q, kc, wuv_pad, *, batch, seq, dims, tq, tk):
    heads, kv_lora = dims["heads"], dims["kv_lora"]
    qk = q.shape[-1]
    nq = seq // tq
    rows = heads * tq
    return pl.pallas_call(
        functools.partial(_mla_kernel, heads=heads, kv_lora=kv_lora, tq=tq, tk=tk),
        grid=(batch, nq),
        in_specs=[pl.BlockSpec((heads, tq, qk), lambda b, i: (0, b * nq + i, 0)),
                  pl.BlockSpec((seq, qk), lambda b, i: (b, 0)),
                  _const_spec(wuv_pad.shape)],
        out_specs=pl.BlockSpec((tq, wuv_pad.shape[-1]), lambda b, i: (b * nq + i, 0)),
        out_shape=jax.ShapeDtypeStruct((batch * seq, wuv_pad.shape[-1]), F32),
        scratch_shapes=[pltpu.VMEM((rows, LANES), F32), pltpu.VMEM((rows, LANES), F32),
                        pltpu.VMEM((rows, kv_lora), F32)],
        compiler_params=_cparams(("parallel", "arbitrary")),
        name="mla_prompt",
    )(q, kc, wuv_pad)


def _log_sigmoids(z):
    sp = jnp.log(1.0 + jnp.exp(-jnp.abs(z)))
    lb = jnp.minimum(z, 0.0) - sp
    return lb, lb - z


def _split_bf16(x):
    hi = x.astype(BF16)
    lo = (x - hi.astype(F32)).astype(BF16)
    return hi, lo


def _sb_kernel(q_ref, k_ref, v_ref, tri_ref, o_ref, acc_ref, car_ref, *, tb, hd):
    i = pl.program_id(2)
    q2 = q_ref[...].astype(F32)
    lane = lax.broadcasted_iota(jnp.int32, q2.shape, 1)
    qs = jnp.concatenate([jnp.where(lane < hd, q2, 0.0), jnp.where(lane >= hd, q2, 0.0)],
                         axis=0).astype(BF16)
    acc_ref[...] = jnp.zeros(acc_ref.shape, F32)
    car_ref[...] = jnp.zeros(car_ref.shape, F32)
    tri = tri_ref[...]

    def step(j, masked):
        off = pl.multiple_of(j * tb, tb)
        k = k_ref[pl.ds(off, tb), :]
        v = v_ref[pl.ds(off, tb), :]
        z = lax.dot_general(qs, k, _NT, preferred_element_type=F32)
        lb, lm = _log_sigmoids(z)
        if masked:
            valid = [(lax.broadcasted_iota(jnp.int32, (2 * tb, LANES), 1) + c * LANES)
                     < (lax.broadcasted_iota(jnp.int32, (2 * tb, LANES), 0) & (tb - 1))
                     for c in range(tb // LANES)]
            lm = jnp.concatenate([jnp.where(valid[c], lm[:, c * LANES:(c + 1) * LANES], 0.0)
                                  for c in range(tb // LANES)], axis=1)
        hi, lo = _split_bf16(lm)
        btw = (jnp.dot(hi, tri, preferred_element_type=F32)
               + jnp.dot(lo, tri, preferred_element_type=F32))
        carry = car_ref[...]
        parts = []
        for c in range(tb // LANES):
            sl = slice(c * LANES, (c + 1) * LANES)
            a = jnp.exp(lb[:, sl] + btw[:, sl] + carry)
            if masked:
                a = jnp.where(valid[c], a, 0.0)
            parts.append(a)
        a = jnp.concatenate(parts, axis=1).astype(BF16)
        acc_ref[...] += jnp.dot(a, v, preferred_element_type=F32)
        car_ref[...] = carry + jnp.sum(lm, axis=1, keepdims=True)

    step(i, True)

    def body(t, carry):
        step(i - 1 - t, False)
        return carry

    lax.fori_loop(0, i, body, 0)
    acc = acc_ref[...]
    lane_o = lax.broadcasted_iota(jnp.int32, (tb, 2 * hd), 1)
    o_ref[...] = jnp.where(lane_o < hd, acc[0:tb], acc[tb:2 * tb])


def _sb_prompt(sqb, skb, svb, tri, *, batch, seq, dims, tb):
    hd = dims["sb_head_dim"]
    width = sqb.shape[-1]
    npair = width // (2 * hd)
    nq = seq // tb
    return pl.pallas_call(
        functools.partial(_sb_kernel, tb=tb, hd=hd),
        grid=(batch, npair, nq),
        in_specs=[pl.BlockSpec((tb, 2 * hd), lambda b, p, i: (b * nq + i, p)),
                  pl.BlockSpec((seq, 2 * hd), lambda b, p, i: (b, p)),
                  pl.BlockSpec((seq, 2 * hd), lambda b, p, i: (b, p)),
                  _const_spec(tri.shape)],
        out_specs=pl.BlockSpec((tb, 2 * hd), lambda b, p, i: (b * nq + i, p)),
        out_shape=jax.ShapeDtypeStruct((batch * seq, width), F32),
        scratch_shapes=[pltpu.VMEM((2 * tb, 2 * hd), F32), pltpu.VMEM((2 * tb, LANES), F32)],
        compiler_params=_cparams(("parallel", "parallel", "arbitrary")),
        name="sb_prompt",
    )(sqb, skb, svb, tri)


def _sample_kernel(pt_ref, qm_ref, qb_ref, ckvn_ref, krn_ref, skn_ref, svn_ref, tri_ref,
                   ckv_hbm, krt_hbm, kt_hbm, vt_hbm,
                   olat_ref, osb_ref,
                   ckv_buf, krt_buf, kt_buf, vt_buf, sem, m_ref, l_ref, accl_ref, accs_ref, car_ref,
                   *, layer, n_pages, ch, page, heads, kv_lora, rope, hd, dec_seq):
    b = pl.program_id(0)
    nb = pl.num_programs(0)
    nch = n_pages // ch
    rows = dec_seq * heads
    keys = ch * page
    nsub = keys // MXU_TILE

    def copies(bb, k, slot):
        c = nch - 1 - k
        out = []
        for j in range(ch):
            pg = pt_ref[bb * n_pages + c * ch + j]
            ks = pl.ds(j * page, page)
            out.append(pltpu.make_async_copy(ckv_hbm.at[layer, pg], ckv_buf.at[slot, ks, :], sem.at[slot, 0]))
            out.append(pltpu.make_async_copy(krt_hbm.at[layer, pg], krt_buf.at[slot, :, ks], sem.at[slot, 1]))
            out.append(pltpu.make_async_copy(kt_hbm.at[layer, pg], kt_buf.at[slot, :, ks], sem.at[slot, 2]))
            out.append(pltpu.make_async_copy(vt_hbm.at[layer, pg], vt_buf.at[slot, :, ks], sem.at[slot, 3]))
        return out

    def start(bb, k, slot):
        for cp in copies(bb, k, slot):
            cp.start()

    def wait(bb, k, slot):
        for cp in copies(bb, k, slot):
            cp.wait()

    @pl.when(b == 0)
    def _():
        start(0, 0, 0)

    qm = qm_ref[0]
    qlat = qm[:, 0:kv_lora]
    qrope = qm[:, kv_lora:kv_lora + rope]
    qb = qb_ref[0]
    tok = lax.broadcasted_iota(jnp.int32, (rows, 1), 0) // heads

    qlat_f = qlat.astype(F32)
    qrope_f = qrope.astype(F32)
    qb_f = qb.astype(F32)
    ckvn = ckvn_ref[0]
    krn = krn_ref[0]
    skn = skn_ref[0]
    svn = svn_ref[0]
    m = jnp.full((rows, 1), NEG_INF, F32)
    l = jnp.zeros((rows, 1), F32)
    accl = jnp.zeros((rows, kv_lora), F32)
    for s in range(dec_seq):
        sc = (jnp.sum(qlat_f * ckvn[s:s + 1, :], axis=1, keepdims=True)
              + jnp.sum(qrope_f * krn[s:s + 1, :], axis=1, keepdims=True))
        sc = jnp.where(tok >= s, sc, NEG_INF)
        m_new = jnp.maximum(m, sc)
        alpha = jnp.exp(m - m_new)
        p = jnp.exp(sc - m_new)
        l = alpha * l + p
        accl = alpha * accl + p * ckvn[s:s + 1, :]
        m = m_new
    carry = jnp.zeros((rows, 1), F32)
    accs = jnp.zeros((rows, heads * hd), F32)
    for s in reversed(range(dec_seq)):
        z = jnp.sum(qb_f * skn[s:s + 1, :], axis=1, keepdims=True)
        lb, lm = _log_sigmoids(z)
        valid = tok > s
        a = jnp.where(valid, jnp.exp(lb + carry), 0.0)
        accs = accs + a * svn[s:s + 1, :]
        carry = carry + jnp.where(valid, lm, 0.0)
    m_ref[...] = m
    l_ref[...] = l
    accl_ref[...] = accl
    accs_ref[...] = accs
    car_ref[...] = carry
    tri = tri_ref[...]

    def compute(slot):
        kc = ckv_buf[slot].astype(BF16)
        krt = krt_buf[slot].astype(BF16)
        s = (lax.dot_general(qlat, kc, _NT, preferred_element_type=F32)
             + jnp.dot(qrope, krt, preferred_element_type=F32))
        m_prev = m_ref[...]
        m_new = jnp.maximum(m_prev, jnp.max(s, axis=1, keepdims=True))
        alpha = jnp.exp(m_prev - m_new)
        p = jnp.exp(s - m_new)
        l_ref[...] = alpha * l_ref[...] + jnp.sum(p, axis=1, keepdims=True)
        m_ref[...] = m_new
        accl_ref[...] = alpha * accl_ref[...] + jnp.dot(p.astype(BF16), kc, preferred_element_type=F32)

        kt = kt_buf[slot].astype(BF16)
        vt = vt_buf[slot].astype(BF16)
        z = jnp.dot(qb, kt, preferred_element_type=F32)
        lb, lm = _log_sigmoids(z)
        hi, lo = _split_bf16(lm)
        stacked = jnp.concatenate(
            [hi[:, u * MXU_TILE:(u + 1) * MXU_TILE] for u in range(nsub)]
            + [lo[:, u * MXU_TILE:(u + 1) * MXU_TILE] for u in range(nsub)], axis=0)
        res = jnp.dot(stacked, tri, preferred_element_type=F32)
        car = car_ref[...]
        parts = [None] * nsub
        for u in reversed(range(nsub)):
            sl = slice(u * MXU_TILE, (u + 1) * MXU_TILE)
            btw = res[u * rows:(u + 1) * rows] + res[(nsub + u) * rows:(nsub + u + 1) * rows] + car
            parts[u] = jnp.exp(lb[:, sl] + btw)
            car = car + jnp.sum(lm[:, sl], axis=1, keepdims=True)
        car_ref[...] = car
        a = jnp.concatenate(parts, axis=1).astype(BF16)
        accs_ref[...] += lax.dot_general(a, vt, _NT, preferred_element_type=F32)

    def pair(kk, carry_):
        k0 = 2 * kk
        start(b, k0 + 1, 1)
        wait(b, k0, 0)
        compute(0)

        @pl.when(k0 + 2 < nch)
        def _():
            start(b, k0 + 2, 0)

        @pl.when(jnp.logical_and(k0 + 2 >= nch, b + 1 < nb))
        def _():
            start(b + 1, 0, 0)

        wait(b, k0 + 1, 1)
        compute(1)
        return carry_

    lax.fori_loop(0, nch // 2, pair, 0)

    olat_ref[0] = accl_ref[...] / l_ref[...]
    accs = accs_ref[...]
    rid = lax.broadcasted_iota(jnp.int32, accs.shape, 0) % heads
    hid = lax.broadcasted_iota(jnp.int32, accs.shape, 1) // hd
    diag = jnp.where(rid == hid, accs, 0.0)
    osb_ref[0] = jnp.sum(diag.reshape(dec_seq, heads, heads * hd), axis=1)


def _sample_attn(pt_flat, qm, qb, ckvn, krn, skn, svn, tri, ckv_c, krt_c, kt_c, vt_c, *, layer, dims, ch):
    db, rows, qk = qm.shape
    heads, kv_lora, rope, hd = dims["heads"], dims["kv_lora"], dims["rope"], dims["sb_head_dim"]
    dec_seq = rows // heads
    page = ckv_c.shape[2]
    n_pages = pt_flat.shape[0] // db
    keys = ch * page
    width = heads * hd

    def per_b(shape):
        nd = len(shape)
        return pl.BlockSpec((1,) + tuple(shape[1:]), lambda b, pt: (b,) + (0,) * (nd - 1))

    anyspec = pl.BlockSpec(memory_space=pl.ANY)
    grid_spec = pltpu.PrefetchScalarGridSpec(
        num_scalar_prefetch=1,
        grid=(db,),
        in_specs=[per_b(qm.shape), per_b(qb.shape), per_b(ckvn.shape), per_b(krn.shape), per_b(skn.shape),
                  per_b(svn.shape), pl.BlockSpec(tri.shape, lambda b, pt: (0, 0)),
                  anyspec, anyspec, anyspec, anyspec],
        out_specs=[pl.BlockSpec((1, rows, kv_lora), lambda b, pt: (b, 0, 0)),
                   pl.BlockSpec((1, dec_seq, width), lambda b, pt: (b, 0, 0))],
        scratch_shapes=[pltpu.VMEM((2, keys, kv_lora), F32), pltpu.VMEM((2, rope, keys), F32),
                        pltpu.VMEM((2, width, keys), F32), pltpu.VMEM((2, width, keys), F32),
                        pltpu.SemaphoreType.DMA((2, 4)),
                        pltpu.VMEM((rows, 1), F32), pltpu.VMEM((rows, 1), F32),
                        pltpu.VMEM((rows, kv_lora), F32), pltpu.VMEM((rows, width), F32),
                        pltpu.VMEM((rows, 1), F32)],
    )
    return pl.pallas_call(
        functools.partial(_sample_kernel, layer=layer, n_pages=n_pages, ch=ch, page=page, heads=heads,
                          kv_lora=kv_lora, rope=rope, hd=hd, dec_seq=dec_seq),
        grid_spec=grid_spec,
        out_shape=[jax.ShapeDtypeStruct((db, rows, kv_lora), F32),
                   jax.ShapeDtypeStruct((db, dec_seq, width), F32)],
        compiler_params=_cparams(("arbitrary",)),
        name="sample_attn",
    )(pt_flat, qm, qb, ckvn, krn, skn, svn, tri, ckv_c, krt_c, kt_c, vt_c)


def _merge_kernel(x_ref, va_ref, osb_ref, sga_ref, sgb_ref, sma_ref, smb_ref, p_ref, wuv_ref, wa_ref, wb_ref,
                  wo_ref, postg_ref, pleg_ref, wpg_ref, wple_ref, y_ref, *, latent_in):
    va = va_ref[...]
    if latent_in:
        va = jnp.dot(va.astype(BF16), wuv_ref[...], preferred_element_type=F32)
    a = (va * sga_ref[...].astype(F32)).astype(BF16)
    bb = (osb_ref[...] * sgb_ref[...].astype(F32)).astype(BF16)
    ya = jnp.dot(a, wa_ref[...], preferred_element_type=F32)
    yb = jnp.dot(bb, wb_ref[...], preferred_element_type=F32)
    mix = (sma_ref[...].astype(F32) * ya + smb_ref[...].astype(F32) * yb).astype(BF16)
    mixed = jnp.dot(mix, wo_ref[...], preferred_element_type=F32)
    h = x_ref[...] + _rms(mixed, postg_ref[...])
    gate = _sigmoid(jnp.dot(_rms(h, pleg_ref[...]).astype(BF16), wpg_ref[...], preferred_element_type=F32))
    emb = jnp.dot(p_ref[...].astype(BF16), wple_ref[...], preferred_element_type=F32)
    y_ref[...] = h + gate * emb


def _merge(x2d, va, osb, sga, sgb, sma, smb, p2d, wuv_bd, wa, wb, wo, post_g, ple_g, wpg, wple, *,
           latent_in, tm):
    rows, d = x2d.shape

    def row(arr):
        return pl.BlockSpec((tm, arr.shape[-1]), lambda i: (i, 0))

    ins = [x2d, va, osb, sga, sgb, sma, smb, p2d]
    consts = [wuv_bd, wa, wb, wo, post_g, ple_g, wpg, wple]
    return pl.pallas_call(
        functools.partial(_merge_kernel, latent_in=latent_in),
        grid=(rows // tm,),
        in_specs=[row(a) for a in ins] + [_const_spec(c.shape) for c in consts],
        out_specs=pl.BlockSpec((tm, d), lambda i: (i, 0)),
        out_shape=jax.ShapeDtypeStruct((rows, d), F32),
        compiler_params=_cparams(("parallel",)),
        name="merge",
    )(*ins, *consts)


def _rope_tables(pos, dr):
    freqs = ROPE_BASE ** (-jnp.arange(0, dr, 2, dtype=F32) / dr)
    ang = pos.astype(F32)[:, None] * freqs[None, :]
    c, s = jnp.cos(ang), jnp.sin(ang)
    pad = jnp.zeros((pos.shape[0], LANES - dr), F32)
    return jnp.concatenate([c, c, pad], axis=1), jnp.concatenate([-s, s, pad], axis=1)


def _swap_halves(w):
    half = w.shape[-1] // 2
    return jnp.concatenate([w[..., half:], w[..., :half]], axis=-1)


def _pad_lanes(w):
    return jnp.pad(w, [(0, 0)] * (w.ndim - 1) + [(0, LANES - w.shape[-1])])


def kernel(x_prompt, x_sample, cache_mla_ckv, cache_mla_krope, cache_sb_k, cache_sb_v, page_table,
           p_prompt, p_sample, pre_norm_g, w_in, q_norm_g, w_uq, kv_norm_g, w_uk, w_uv,
           w_branch_a, w_branch_b, w_out, post_norm_g, ple_norm_g, w_ple_gate, w_ple):
    batch, seq, d = x_prompt.shape
    db, dec_seq, _ = x_sample.shape
    depth = w_in.shape[0]
    q_lora = q_norm_g.shape[1]
    kv_lora, heads, nope = w_uk.shape[1:]
    vdim = w_uv.shape[3]
    rope = cache_mla_krope.shape[3]
    n_phys, page, sb_heads, hd = cache_sb_k.shape[1:]
    n_pages = page_table.shape[1]
    past_len = n_pages * page
    mla_width = heads * vdim
    sb_width = sb_heads * hd
    assert sb_heads == heads and mla_width == sb_width and 2 * hd == LANES and rope <= LANES
    mla_scale = 1.0 / math.sqrt(nope + rope)
    sb_scale = 1.0 / math.sqrt(hd)
    dims = dict(heads=heads, kv_lora=kv_lora, rope=rope, mla_width=mla_width, sb_width=sb_width,
                sb_head_dim=hd)

    sizes = (q_lora, kv_lora, rope, mla_width, sb_width, sb_width, sb_width, sb_width, d, d)
    offs = [0]
    for n in sizes:
        offs.append(offs[-1] + n)
    names = ("cq", "ckv", "kra", "krb", "ga", "sq", "sk", "sv", "gb", "ma", "mb")
    widths = (q_lora, kv_lora, LANES, LANES, mla_width, sb_width, sb_width, sb_width, sb_width, d, d)
    cols, lo = {}, 0
    for n, w in zip(names, widths):
        cols[n] = (lo, lo + w)
        lo += w

    pos_p = jnp.arange(seq, dtype=jnp.int32)
    pos_s = past_len + (jnp.arange(db * dec_seq, dtype=jnp.int32) % dec_seq)
    cos_p, sin_p = _rope_tables(pos_p, rope)
    cos_s, sin_s = _rope_tables(pos_s, rope)
    tri = (lax.broadcasted_iota(jnp.int32, (MXU_TILE, MXU_TILE), 0)
           > lax.broadcasted_iota(jnp.int32, (MXU_TILE, MXU_TILE), 1)).astype(BF16)

    krt_c = jnp.transpose(cache_mla_krope, (0, 1, 3, 2))
    kt_c = jnp.transpose(cache_sb_k, (0, 1, 3, 4, 2)).reshape(depth, n_phys, sb_width, page)
    vt_c = jnp.transpose(cache_sb_v, (0, 1, 3, 4, 2)).reshape(depth, n_phys, sb_width, page)
    pt_flat = page_table.reshape(-1)
    head_mask = (jnp.arange(heads)[:, None] == (jnp.arange(sb_width) // hd)[None, :])

    hp = x_prompt.reshape(batch * seq, d)
    hs = x_sample.reshape(db * dec_seq, d)
    outs = [[] for _ in range(8)]
    tm_p = 256
    for i in range(depth):
        w = w_in[i]
        parts = [w[:, offs[k]:offs[k + 1]] for k in range(len(sizes))]
        cq_w, ckv_w, kr_w, ga_w, sq_w, sk_w, sv_w, gb_w, ma_w, mb_w = parts
        w_in2 = jnp.concatenate(
            [cq_w, ckv_w, _pad_lanes(kr_w), _pad_lanes(_swap_halves(kr_w)), ga_w, sq_w * sb_scale, sk_w, sv_w,
             gb_w, ma_w, mb_w], axis=1).astype(BF16)
        uq = w_uq[i].reshape(q_lora, heads, nope + rope)
        uq_rope = uq[:, :, nope:] * mla_scale
        w_qr = _pad_lanes(uq_rope).reshape(q_lora, heads * LANES).astype(BF16)
        w_qrs = _pad_lanes(_swap_halves(uq_rope)).reshape(q_lora, heads * LANES).astype(BF16)
        w_qlat = _fold_qlat(jnp.transpose(uq[:, :, :nope], (1, 0, 2)), jnp.transpose(w_uk[i], (1, 0, 2)),
                            mla_scale)
        uv = jnp.transpose(w_uv[i], (1, 0, 2))
        wuv_pad = (uv[:, :, None, :] * jnp.eye(heads, dtype=F32)[:, None, :, None]).reshape(
            heads, kv_lora, mla_width).astype(BF16)
        wuv_bd = wuv_pad.reshape(heads * kv_lora, mla_width)
        wa, wb, wo = w_branch_a[i].astype(BF16), w_branch_b[i].astype(BF16), w_out[i].astype(BF16)
        wpg, wple = w_ple_gate[i].astype(BF16), w_ple[i].astype(BF16)
        pre_g, q_g, kv_g = pre_norm_g[i][None], q_norm_g[i][None], kv_norm_g[i][None]
        post_g, ple_g = post_norm_g[i][None], ple_norm_g[i][None]
        proj = functools.partial(_proj, pre_g=pre_g, q_g=q_g, kv_g=kv_g, w_in2=w_in2, w_qlat=w_qlat, w_qr=w_qr,
                                 w_qrs=w_qrs, cols=cols, dims=dims)
        merge = functools.partial(_merge, wuv_bd=wuv_bd, wa=wa, wb=wb, wo=wo, post_g=post_g, ple_g=ple_g,
                                  wpg=wpg, wple=wple)

        (q, kc, ckv, krope, sga, sqb, skb, svb, sk, sv, sgb, sma, smb) = proj(hp, cos_p, sin_p, tm=tm_p)
        va = _mla_prompt(q, kc, wuv_pad, batch=batch, seq=seq, dims=dims, tq=128, tk=512)
        osb = _sb_prompt(sqb, skb, svb, tri, batch=batch, seq=seq, dims=dims, tb=MXU_TILE)
        hp = merge(hp, va, osb, sga, sgb, sma, smb, p_prompt[i].reshape(batch * seq, -1),
                   latent_in=False, tm=tm_p)
        outs[0].append(ckv.reshape(batch, seq, kv_lora))
        outs[1].append(krope.reshape(batch, seq, rope))
        outs[2].append(sk.reshape(batch, seq, sb_heads, hd))
        outs[3].append(sv.reshape(batch, seq, sb_heads, hd))

        rows_s = db * dec_seq
        (q, kc, ckv, krope, sga, sqb, skb, svb, sk, sv, sgb, sma, smb) = proj(hs, cos_s, sin_s, tm=rows_s)
        qm = jnp.transpose(q.reshape(heads, db, dec_seq, -1), (1, 2, 0, 3)).reshape(db, dec_seq * heads, -1)
        qb = jnp.where(head_mask[None, None], sqb.reshape(db, dec_seq, 1, sb_width),
                       jnp.zeros((), BF16)).reshape(db, dec_seq * heads, sb_width)
        olat, osb = _sample_attn(
            pt_flat, qm, qb, ckv.reshape(db, dec_seq, kv_lora), krope.reshape(db, dec_seq, rope),
            sk.reshape(db, dec_seq, sb_width), sv.reshape(db, dec_seq, sb_width), tri,
            cache_mla_ckv, krt_c, kt_c, vt_c, layer=i, dims=dims, ch=8)
        hs = merge(hs, olat.reshape(rows_s, heads * kv_lora), osb.reshape(rows_s, sb_width), sga, sgb, sma, smb,
                   p_sample[i].reshape(rows_s, -1), latent_in=True, tm=rows_s)
        outs[4].append(ckv.reshape(db, dec_seq, kv_lora))
        outs[5].append(krope.reshape(db, dec_seq, rope))
        outs[6].append(sk.reshape(db, dec_seq, sb_heads, hd))
        outs[7].append(sv.reshape(db, dec_seq, sb_heads, hd))

    return (hp.reshape(batch, seq, d), hs.reshape(db, dec_seq, d)) + tuple(jnp.stack(o, axis=0) for o in outs)
```

```python
import functools
import math

import jax
import jax.numpy as jnp
from jax import lax
from jax.experimental import pallas as pl
from jax.experimental.pallas import tpu as pltpu

F32 = jnp.float32
BF16 = jnp.bfloat16
EPS = 1e-6
ROPE_BASE = 10000.0
NEG_INF = -1e30
LOG2E = math.log2(math.e)
LANES = 128
MXU_TILE = 256
VMEM_LIMIT = 56 * 1024 * 1024

_NT = (((1,), (1,)), ((), ()))


def _cparams(sems):
    return pltpu.CompilerParams(dimension_semantics=sems, vmem_limit_bytes=VMEM_LIMIT)


def _rms(x, g):
    return x * lax.rsqrt(jnp.mean(x * x, axis=-1, keepdims=True) + EPS) * g


def _sigmoid(x):
    return jax.nn.sigmoid(x)


def _const_spec(shape):
    nd = len(shape)
    return pl.BlockSpec(shape, lambda *_: (0,) * nd)


def _softplus2(z2):
    return jnp.maximum(z2, 0.0) + jnp.log(1.0 + jnp.exp2(-jnp.abs(z2))) * LOG2E


def _split_bf16(x):
    hi = x.astype(BF16)
    lo = (x - hi.astype(F32)).astype(BF16)
    return hi, lo


def _fold_kernel(a_ref, b_ref, o_ref, *, scale):
    w = lax.dot_general(a_ref[0], b_ref[0], _NT, precision=lax.Precision.HIGHEST,
                        preferred_element_type=F32)
    o_ref[...] = (w * scale).astype(o_ref.dtype)


def _fold_qlat(nope_t, uk_t, scale):
    h, ql, n = nope_t.shape
    r = uk_t.shape[1]
    return pl.pallas_call(
        functools.partial(_fold_kernel, scale=scale),
        grid=(h,),
        in_specs=[pl.BlockSpec((1, ql, n), lambda i: (i, 0, 0)),
                  pl.BlockSpec((1, r, n), lambda i: (i, 0, 0))],
        out_specs=pl.BlockSpec((ql, r), lambda i: (0, i)),
        out_shape=jax.ShapeDtypeStruct((ql, h * r), BF16),
        compiler_params=_cparams(("arbitrary",)),
        name="fold_qlat",
    )(nope_t, uk_t)


def _proj_kernel(x_ref, cos0_ref, sin0_ref, cosh_ref, sinh_ref, preg_ref, qg_ref, kvg_ref, win_ref,
                 wq_ref, wqs_ref, wk_ref, wv_ref, *out_refs, cols, heads, kv_lora, rope, absorbed):
    (ckv_ref, kr_ref, sga_ref, sqb_ref, sk_ref, sv_ref, sgb_ref, sma_ref, smb_ref) = out_refs[:9]
    xn = _rms(x_ref[...], preg_ref[...]).astype(BF16)

    def proj(name):
        lo, hi = cols[name]
        return jnp.dot(xn, win_ref[:, lo:hi], preferred_element_type=F32)

    cqn = _rms(proj("cq"), qg_ref[...]).astype(BF16)
    ckvn = _rms(proj("ckv"), kvg_ref[...])
    krope = proj("kra") * cos0_ref[...] + proj("krb") * sin0_ref[...]
    ckv_ref[...] = ckvn
    kr_ref[...] = krope[:, :rope]

    if absorbed:
        (q_ref,) = out_refs[9:]
        cos, sin = cos0_ref[...], sin0_ref[...]
        qlat = jnp.dot(cqn, wk_ref[...], preferred_element_type=F32)
        qr = jnp.dot(cqn, wq_ref[...], preferred_element_type=F32)
        qrs = jnp.dot(cqn, wqs_ref[...], preferred_element_type=F32)
        for h in range(heads):
            q_ref[h, :, 0:kv_lora] = qlat[:, h * kv_lora:(h + 1) * kv_lora].astype(BF16)
            sl = slice(h * LANES, (h + 1) * LANES)
            q_ref[h, :, kv_lora:kv_lora + LANES] = (qr[:, sl] * cos + qrs[:, sl] * sin).astype(BF16)
    else:
        (q_ref, k_ref, v_ref, skb_ref, svb_ref) = out_refs[9:]
        cos, sin = cosh_ref[...], sinh_ref[...]
        ckvb = ckvn.astype(BF16)
        knope = jnp.dot(ckvb, wk_ref[...], preferred_element_type=F32)
        krh = proj("krah") * cos + proj("krbh") * sin
        qn = jnp.dot(cqn, wq_ref[...], preferred_element_type=F32)
        qs = jnp.dot(cqn, wqs_ref[...], preferred_element_type=F32)
        for h in range(heads):
            sl = slice(h * LANES, (h + 1) * LANES)
            k_ref[:, sl] = (knope[:, sl] + krh).astype(BF16)
            q_ref[:, sl] = (qn[:, sl] * cos + qs[:, sl] * sin).astype(BF16)
        v_ref[...] = jnp.dot(ckvb, wv_ref[...], preferred_element_type=F32).astype(BF16)

    ga = proj("ga")
    sga_ref[...] = (ga * _sigmoid(ga)).astype(BF16)
    sqb_ref[...] = proj("sq").astype(BF16)
    sk = proj("sk")
    sk_ref[...] = sk
    sv = proj("sv")
    sv_ref[...] = sv
    if not absorbed:
        skb_ref[...] = sk.astype(BF16)
        svb_ref[...] = sv.astype(BF16)
    gb = proj("gb")
    sgb_ref[...] = (gb * _sigmoid(gb)).astype(BF16)
    sma_ref[...] = _sigmoid(proj("ma")).astype(BF16)
    smb_ref[...] = _sigmoid(proj("mb")).astype(BF16)


def _proj(x2d, tabs, pre_g, q_g, kv_g, w_in2, wq, wqs, wk, wv, *, cols, dims, tm, absorbed):
    rows, d = x2d.shape
    heads, kv_lora, rope = dims["heads"], dims["kv_lora"], dims["rope"]
    a_w, b_w = dims["mla_width"], dims["sb_width"]
    nblk = rows // tm
    ntab = tabs[0].shape[0] // tm

    def row(n):
        return pl.BlockSpec((tm, n), lambda i: (i, 0))

    tab = pl.BlockSpec((tm, LANES), lambda i: (i % ntab, 0))
    consts = [pre_g, q_g, kv_g, w_in2, wq, wqs, wk, wv]
    in_specs = [row(d), tab, tab, tab, tab] + [_const_spec(c.shape) for c in consts]
    out_shape = [
        jax.ShapeDtypeStruct((rows, kv_lora), F32),
        jax.ShapeDtypeStruct((rows, rope), F32),
        jax.ShapeDtypeStruct((rows, a_w), BF16),
        jax.ShapeDtypeStruct((rows, b_w), BF16),
        jax.ShapeDtypeStruct((rows, b_w), F32),
        jax.ShapeDtypeStruct((rows, b_w), F32),
        jax.ShapeDtypeStruct((rows, b_w), BF16),
        jax.ShapeDtypeStruct((rows, d), BF16),
        jax.ShapeDtypeStruct((rows, d), BF16),
    ]
    out_specs = [row(kv_lora), row(rope), row(a_w), row(b_w), row(b_w), row(b_w), row(b_w), row(d), row(d)]
    if absorbed:
        qk = kv_lora + LANES
        out_shape.append(jax.ShapeDtypeStruct((heads, rows, qk), BF16))
        out_specs.append(pl.BlockSpec((heads, tm, qk), lambda i: (0, i, 0)))
    else:
        hw = heads * LANES
        out_shape += [jax.ShapeDtypeStruct((rows, hw), BF16), jax.ShapeDtypeStruct((rows, hw), BF16),
                      jax.ShapeDtypeStruct((rows, a_w), BF16), jax.ShapeDtypeStruct((rows, b_w), BF16),
                      jax.ShapeDtypeStruct((rows, b_w), BF16)]
        out_specs += [row(hw), row(hw), row(a_w), row(b_w), row(b_w)]
    return pl.pallas_call(
        functools.partial(_proj_kernel, cols=cols, heads=heads, kv_lora=kv_lora, rope=rope, absorbed=absorbed),
        grid=(nblk,),
        in_specs=in_specs,
        out_specs=out_specs,
        out_shape=out_shape,
        compiler_params=_cparams(("parallel",)),
        name="proj_sample" if absorbed else "proj_prompt",
    )(x2d, *tabs, *consts)


def _mla_kernel(q_ref, k_ref, v_ref, o_ref, m_ref, l_ref, acc_ref, *, hps, vd, tq, tk):
    i = pl.program_id(2)
    rows = hps * tq
    qs = [q_ref[:, h * LANES:(h + 1) * LANES] for h in range(hps)]
    m_ref[...] = jnp.full(m_ref.shape, NEG_INF, F32)
    l_ref[...] = jnp.zeros(l_ref.shape, F32)
    acc_ref[...] = jnp.zeros(acc_ref.shape, F32)
    nfull = (i * tq) // tk

    def step(j, masked):
        off = pl.multiple_of(j * tk, tk)
        k = k_ref[pl.ds(off, tk), :]
        v = v_ref[pl.ds(off, tk), :]
        s = jnp.concatenate(
            [lax.dot_general(qs[h], k[:, h * LANES:(h + 1) * LANES], _NT, preferred_element_type=F32)
             for h in range(hps)], axis=0)
        m_prev = m_ref[...]
        if masked:
            qpos = i * tq + (lax.broadcasted_iota(jnp.int32, (rows, LANES), 0) & (tq - 1))
            kpos = j * tk + lax.broadcasted_iota(jnp.int32, (rows, LANES), 1)
            s = jnp.concatenate(
                [jnp.where(kpos + c * LANES <= qpos, s[:, c * LANES:(c + 1) * LANES], NEG_INF)
                 for c in range(tk // LANES)], axis=1)
        m_new = jnp.maximum(m_prev, jnp.max(s, axis=1, keepdims=True))
        alpha = jnp.exp2(m_prev - m_new)
        p = jnp.concatenate(
            [jnp.exp2(s[:, c * LANES:(c + 1) * LANES] - m_new) for c in range(tk // LANES)], axis=1)
        l_ref[...] = alpha * l_ref[...] + jnp.sum(p, axis=1, keepdims=True)
        m_ref[...] = m_new
        pb = p.astype(BF16)
        pv = jnp.concatenate(
            [jnp.dot(pb[g * 2 * tq:(g + 1) * 2 * tq], v[:, g * LANES:(g + 1) * LANES],
                     preferred_element_type=F32) for g in range(hps // 2)], axis=0)
        acc_ref[...] = alpha * acc_ref[...] + pv

    def body(j, carry):
        step(j, False)
        return carry

    lax.fori_loop(0, nfull, body, 0)
    step(nfull, True)

    o = acc_ref[...] / l_ref[...]
    lane = lax.broadcasted_iota(jnp.int32, (tq, LANES), 1)
    for g in range(hps // 2):
        lo = o[(2 * g) * tq:(2 * g + 1) * tq]
        hi = o[(2 * g + 1) * tq:(2 * g + 2) * tq]
        o_ref[:, g * LANES:(g + 1) * LANES] = jnp.where(lane < vd, lo, hi)


def _mla_prompt(q, k, v, *, batch, seq, dims, hps, tq, tk):
    heads = dims["heads"]
    vd = v.shape[-1] // heads
    assert 2 * vd == LANES and hps % 2 == 0
    nq = seq // tq
    return pl.pallas_call(
        functools.partial(_mla_kernel, hps=hps, vd=vd, tq=tq, tk=tk),
        grid=(batch, heads // hps, nq),
        in_specs=[pl.BlockSpec((tq, hps * LANES), lambda b, g, i: (b * nq + i, g)),
                  pl.BlockSpec((seq, hps * LANES), lambda b, g, i: (b, g)),
                  pl.BlockSpec((seq, hps * vd), lambda b, g, i: (b, g))],
        out_specs=pl.BlockSpec((tq, hps * vd), lambda b, g, i: (b * nq + i, g)),
        out_shape=jax.ShapeDtypeStruct((batch * seq, heads * vd), F32),
        scratch_shapes=[pltpu.VMEM((hps * tq, LANES), F32), pltpu.VMEM((hps * tq, LANES), F32),
                        pltpu.VMEM((hps * tq, LANES), F32)],
        compiler_params=_cparams(("parallel", "parallel", "arbitrary")),
        name="mla_prompt",
    )(q, k, v)


def _sb_kernel(q_ref, k_ref, v_ref, tri_ref, o_ref, acc_ref, car_ref, pend_ref, z_ref, *, tb, hd, npair):
    i = pl.program_id(2)
    lane = lax.broadcasted_iota(jnp.int32, (tb, LANES), 1)
    qs = []
    for p in range(npair):
        q2 = q_ref[:, p * LANES:(p + 1) * LANES].astype(F32)
        qs.append(jnp.concatenate([jnp.where(lane < hd, q2, 0.0), jnp.where(lane >= hd, q2, 0.0)],
                                  axis=0).astype(BF16))
    acc_ref[...] = jnp.zeros(acc_ref.shape, F32)
    car_ref[...] = jnp.zeros(car_ref.shape, F32)
    pend_ref[...] = jnp.zeros(pend_ref.shape, F32)

    def scores(j):
        off = pl.multiple_of(j * tb, tb)
        return [lax.dot_general(qs[p], k_ref[pl.ds(off, tb), p * LANES:(p + 1) * LANES], _NT,
                                preferred_element_type=F32) for p in range(npair)]

    def step(j, masked, zs):
        off = pl.multiple_of(j * tb, tb)
        tri = tri_ref[...]
        if masked:
            valid = [(lax.broadcasted_iota(jnp.int32, (2 * tb, LANES), 1) + c * LANES)
                     < (lax.broadcasted_iota(jnp.int32, (2 * tb, LANES), 0) & (tb - 1))
                     for c in range(tb // LANES)]
        pairs = range(npair)
        carry = car_ref[...] - jnp.sum(pend_ref[...], axis=1, keepdims=True)
        vs = [v_ref[pl.ds(off, tb), p * LANES:(p + 1) * LANES] for p in pairs]
        sps = [_softplus2(z) for z in zs]
        if masked:
            sps = [jnp.concatenate([jnp.where(valid[c], sp[:, c * LANES:(c + 1) * LANES], 0.0)
                                    for c in range(tb // LANES)], axis=1) for sp in sps]
        incls = [jnp.dot(jnp.concatenate(_split_bf16(sp), axis=1), tri, preferred_element_type=F32)
                 for sp in sps]
        pvs = []
        for p in pairs:
            cp = carry[p * 2 * tb:(p + 1) * 2 * tb]
            parts = []
            for c in range(tb // LANES):
                sl = slice(c * LANES, (c + 1) * LANES)
                a = jnp.exp2(zs[p][:, sl] + incls[p][:, sl] + cp)
                if masked:
                    a = jnp.where(valid[c], a, 0.0)
                parts.append(a)
            a = jnp.concatenate(parts, axis=1).astype(BF16)
            pvs.append(jnp.dot(a, vs[p], preferred_element_type=F32))
        acc_ref[...] += jnp.concatenate(pvs, axis=0)
        car_ref[...] = carry
        pend_ref[...] = jnp.concatenate(
            [functools.reduce(lambda x, y: x + y, [sp[:, c * LANES:(c + 1) * LANES] for c in range(tb // LANES)])
             for sp in sps], axis=0)

    z_diag = scores(i)
    z_ref[...] = jnp.concatenate(scores(jnp.maximum(i - 1, 0)), axis=0)
    step(i, True, z_diag)

    def body(t, carry):
        j = i - 1 - t
        z_all = z_ref[...]
        z_next = jnp.concatenate(scores(jnp.maximum(j - 1, 0)), axis=0)
        step(j, False, [z_all[p * 2 * tb:(p + 1) * 2 * tb] for p in range(npair)])
        z_ref[...] = z_next
        return carry

    lax.fori_loop(0, i, body, 0)
    for p in range(npair):
        o_ref[:, p * LANES:(p + 1) * LANES] = jnp.where(
            lane < hd, acc_ref[(2 * p) * tb:(2 * p + 1) * tb], acc_ref[(2 * p + 1) * tb:(2 * p + 2) * tb])


def _sb_prompt(sqb, skb, svb, tri, *, batch, seq, dims, tb, npair):
    hd = dims["sb_head_dim"]
    width = sqb.shape[-1]
    bw = npair * 2 * hd
    nq = seq // tb
    return pl.pallas_call(
        functools.partial(_sb_kernel, tb=tb, hd=hd, npair=npair),
        grid=(batch, width // bw, nq),
        in_specs=[pl.BlockSpec((tb, bw), lambda b, g, i: (b * nq + i, g)),
                  pl.BlockSpec((seq, bw), lambda b, g, i: (b, g)),
                  pl.BlockSpec((seq, bw), lambda b, g, i: (b, g)),
                  _const_spec(tri.shape)],
        out_specs=pl.BlockSpec((tb, bw), lambda b, g, i: (b * nq + i, g)),
        out_shape=jax.ShapeDtypeStruct((batch * seq, width), F32),
        scratch_shapes=[pltpu.VMEM((npair * 2 * tb, 2 * hd), F32), pltpu.VMEM((npair * 2 * tb, LANES), F32),
                        pltpu.VMEM((npair * 2 * tb, LANES), F32), pltpu.VMEM((npair * 2 * tb, tb), F32)],
        compiler_params=_cparams(("parallel", "parallel", "arbitrary")),
        name="sb_prompt",
    )(sqb, skb, svb, tri)


def _sample_kernel(pt_ref, qm_ref, qb_ref, ckvn_ref, krn_ref, skn_ref, svn_ref, tri_ref,
                   ckv_hbm, krt_hbm, kt_hbm, vt_hbm,
                   olat_ref, osb_ref,
                   ckv_buf, krt_buf, kt_buf, vt_buf, sem, m_ref, l_ref, accl_ref, accs_ref, car_ref,
                   *, layer, n_pages, ch, page, nslot, heads, kv_lora, rope, hd, dec_seq):
    b = pl.program_id(0)
    nb = pl.num_programs(0)
    nch = n_pages // ch
    ahead = nslot - 1
    rows = dec_seq * heads
    keys = ch * page
    nsub = keys // MXU_TILE

    def copies(bb, k, slot):
        c = nch - 1 - k
        out = []
        for j in range(ch):
            pg = pt_ref[bb * n_pages + c * ch + j]
            ks = pl.ds(j * page, page)
            out.append(pltpu.make_async_copy(ckv_hbm.at[layer, pg], ckv_buf.at[slot, ks, :], sem.at[slot, 0]))
            out.append(pltpu.make_async_copy(krt_hbm.at[layer, pg], krt_buf.at[slot, :, ks], sem.at[slot, 1]))
            out.append(pltpu.make_async_copy(kt_hbm.at[layer, pg], kt_buf.at[slot, :, ks], sem.at[slot, 2]))
            out.append(pltpu.make_async_copy(vt_hbm.at[layer, pg], vt_buf.at[slot, :, ks], sem.at[slot, 3]))
        return out

    def start(bb, k, slot):
        for cp in copies(bb, k, slot):
            cp.start()

    def wait(bb, k, slot):
        for cp in copies(bb, k, slot):
            cp.wait()

    @pl.when(b == 0)
    def _():
        for k in range(ahead):
            start(0, k, k)

    qm = qm_ref[0]
    qlat = qm[:, 0:kv_lora]
    qrope = qm[:, kv_lora:kv_lora + rope]
    qb = qb_ref[0]
    tok = lax.broadcasted_iota(jnp.int32, (rows, 1), 0) // heads

    qlat_f = qlat.astype(F32)
    qrope_f = qrope.astype(F32)
    qb_f = qb.astype(F32)
    ckvn = ckvn_ref[0]
    krn = krn_ref[0]
    skn = skn_ref[0]
    svn = svn_ref[0]
    m = jnp.full((rows, 1), NEG_INF, F32)
    l = jnp.zeros((rows, 1), F32)
    accl = jnp.zeros((rows, kv_lora), F32)
    for s in range(dec_seq):
        sc = (jnp.sum(qlat_f * ckvn[s:s + 1, :], axis=1, keepdims=True)
              + jnp.sum(qrope_f * krn[s:s + 1, :], axis=1, keepdims=True))
        sc = jnp.where(tok >= s, sc, NEG_INF)
        m_new = jnp.maximum(m, sc)
        alpha = jnp.exp2(m - m_new)
        p = jnp.exp2(sc - m_new)
        l = alpha * l + p
        accl = alpha * accl + p * ckvn[s:s + 1, :]
        m = m_new
    carry = jnp.zeros((rows, 1), F32)
    accs = jnp.zeros((rows, heads * hd), F32)
    for s in reversed(range(dec_seq)):
        z = jnp.sum(qb_f * skn[s:s + 1, :], axis=1, keepdims=True)
        sp = _softplus2(z)
        valid = tok > s
        a = jnp.where(valid, jnp.exp2(z - sp + carry), 0.0)
        accs = accs + a * svn[s:s + 1, :]
        carry = carry - jnp.where(valid, sp, 0.0)
    m_ref[...] = m
    l_ref[...] = l
    accl_ref[...] = accl
    accs_ref[...] = accs
    car_ref[...] = carry

    def compute(slot):
        kc = ckv_buf[slot].astype(BF16)
        krt = krt_buf[slot].astype(BF16)
        s = (lax.dot_general(qlat, kc, _NT, preferred_element_type=F32)
             + jnp.dot(qrope, krt, preferred_element_type=F32))
        m_prev = m_ref[...]
        m_new = jnp.maximum(m_prev, jnp.max(s, axis=1, keepdims=True))
        alpha = jnp.exp2(m_prev - m_new)
        p = jnp.exp2(s - m_new)
        l_ref[...] = alpha * l_ref[...] + jnp.sum(p, axis=1, keepdims=True)
        m_ref[...] = m_new
        accl_ref[...] = alpha * accl_ref[...] + jnp.dot(p.astype(BF16), kc, preferred_element_type=F32)

        kt = kt_buf[slot].astype(BF16)
        vt = vt_buf[slot].astype(BF16)
        z = jnp.dot(qb, kt, preferred_element_type=F32)
        sp = _softplus2(z)
        hi, lo = _split_bf16(sp)
        stacked = jnp.concatenate(
            [jnp.concatenate([hi[:, u * MXU_TILE:(u + 1) * MXU_TILE], lo[:, u * MXU_TILE:(u + 1) * MXU_TILE]],
                             axis=1) for u in range(nsub)], axis=0)
        incl = jnp.dot(stacked, tri_ref[...], preferred_element_type=F32)
        car = car_ref[...]
        parts = [None] * nsub
        for u in reversed(range(nsub)):
            sl = slice(u * MXU_TILE, (u + 1) * MXU_TILE)
            parts[u] = jnp.exp2(z[:, sl] + incl[u * rows:(u + 1) * rows] + car)
            car = car - jnp.sum(sp[:, sl], axis=1, keepdims=True)
        car_ref[...] = car
        a = jnp.concatenate(parts, axis=1).astype(BF16)
        accs_ref[...] += lax.dot_general(a, vt, _NT, preferred_element_type=F32)

    def group(it, carry_):
        for u in range(nslot):
            k = it * nslot + u
            kn = k + ahead
            bn = b + kn // nch

            @pl.when(bn < nb)
            def _():
                start(bn, kn % nch, (u + ahead) % nslot)

            wait(b, k, u)
            compute(u)
        return carry_

    lax.fori_loop(0, nch // nslot, group, 0)

    olat_ref[0] = accl_ref[...] / l_ref[...]
    accs = accs_ref[...]
    rid = lax.broadcasted_iota(jnp.int32, accs.shape, 0) % heads
    hid = lax.broadcasted_iota(jnp.int32, accs.shape, 1) // hd
    diag = jnp.where(rid == hid, accs, 0.0)
    osb_ref[0] = jnp.sum(diag.reshape(dec_seq, heads, heads * hd), axis=1)


def _sample_attn(pt_flat, qm, qb, ckvn, krn, skn, svn, tri, ckv_c, krt_c, kt_c, vt_c, *, layer, dims, ch, nslot):
    db, rows, qk = qm.shape
    heads, kv_lora, rope, hd = dims["heads"], dims["kv_lora"], dims["rope"], dims["sb_head_dim"]
    dec_seq = rows // heads
    page = ckv_c.shape[2]
    n_pages = pt_flat.shape[0] // db
    assert n_pages % (ch * nslot) == 0
    keys = ch * page
    width = heads * hd

    def per_b(shape):
        nd = len(shape)
        return pl.BlockSpec((1,) + tuple(shape[1:]), lambda b, pt: (b,) + (0,) * (nd - 1))

    anyspec = pl.BlockSpec(memory_space=pl.ANY)
    grid_spec = pltpu.PrefetchScalarGridSpec(
        num_scalar_prefetch=1,
        grid=(db,),
        in_specs=[per_b(qm.shape), per_b(qb.shape), per_b(ckvn.shape), per_b(krn.shape), per_b(skn.shape),
                  per_b(svn.shape), pl.BlockSpec(tri.shape, lambda b, pt: (0, 0)),
                  anyspec, anyspec, anyspec, anyspec],
        out_specs=[pl.BlockSpec((1, rows, kv_lora), lambda b, pt: (b, 0, 0)),
                   pl.BlockSpec((1, dec_seq, width), lambda b, pt: (b, 0, 0))],
        scratch_shapes=[pltpu.VMEM((nslot, keys, kv_lora), F32), pltpu.VMEM((nslot, rope, keys), F32),
                        pltpu.VMEM((nslot, width, keys), F32), pltpu.VMEM((nslot, width, keys), F32),
                        pltpu.SemaphoreType.DMA((nslot, 4)),
                        pltpu.VMEM((rows, 1), F32), pltpu.VMEM((rows, 1), F32),
                        pltpu.VMEM((rows, kv_lora), F32), pltpu.VMEM((rows, width), F32),
                        pltpu.VMEM((rows, 1), F32)],
    )
    return pl.pallas_call(
        functools.partial(_sample_kernel, layer=layer, n_pages=n_pages, ch=ch, page=page, nslot=nslot,
                          heads=heads, kv_lora=kv_lora, rope=rope, hd=hd, dec_seq=dec_seq),
        grid_spec=grid_spec,
        out_shape=[jax.ShapeDtypeStruct((db, rows, kv_lora), F32),
                   jax.ShapeDtypeStruct((db, dec_seq, width), F32)],
        compiler_params=_cparams(("arbitrary",)),
        name="sample_attn",
    )(pt_flat, qm, qb, ckvn, krn, skn, svn, tri, ckv_c, krt_c, kt_c, vt_c)


def _merge_kernel(x_ref, va_ref, osb_ref, sga_ref, sgb_ref, sma_ref, smb_ref, p_ref, wuv_ref, wa_ref, wb_ref,
                  wo_ref, postg_ref, pleg_ref, wpg_ref, wple_ref, y_ref, *, latent_in):
    va = va_ref[...]
    if latent_in:
        va = jnp.dot(va.astype(BF16), wuv_ref[...], preferred_element_type=F32)
    a = (va * sga_ref[...].astype(F32)).astype(BF16)
    bb = (osb_ref[...] * sgb_ref[...].astype(F32)).astype(BF16)
    ya = jnp.dot(a, wa_ref[...], preferred_element_type=F32)
    yb = jnp.dot(bb, wb_ref[...], preferred_element_type=F32)
    mix = (sma_ref[...].astype(F32) * ya + smb_ref[...].astype(F32) * yb).astype(BF16)
    mixed = jnp.dot(mix, wo_ref[...], preferred_element_type=F32)
    h = x_ref[...] + _rms(mixed, postg_ref[...])
    gate = _sigmoid(jnp.dot(_rms(h, pleg_ref[...]).astype(BF16), wpg_ref[...], preferred_element_type=F32))
    emb = jnp.dot(p_ref[...].astype(BF16), wple_ref[...], preferred_element_type=F32)
    y_ref[...] = h + gate * emb


def _merge(x2d, va, osb, sga, sgb, sma, smb, p2d, wuv_bd, wa, wb, wo, post_g, ple_g, wpg, wple, *,
           latent_in, tm):
    rows, d = x2d.shape

    def row(arr):
        return pl.BlockSpec((tm, arr.shape[-1]), lambda i: (i, 0))

    ins = [x2d, va, osb, sga, sgb, sma, smb, p2d]
    consts = [wuv_bd, wa, wb, wo, post_g, ple_g, wpg, wple]
    return pl.pallas_call(
        functools.partial(_merge_kernel, latent_in=latent_in),
        grid=(rows // tm,),
        in_specs=[row(a) for a in ins] + [_const_spec(c.shape) for c in consts],
        out_specs=pl.BlockSpec((tm, d), lambda i: (i, 0)),
        out_shape=jax.ShapeDtypeStruct((rows, d), F32),
        compiler_params=_cparams(("parallel",)),
        name="merge",
    )(*ins, *consts)


def _rope_tables(pos, dr, lead):
    freqs = ROPE_BASE ** (-jnp.arange(0, dr, 2, dtype=F32) / dr)
    ang = pos.astype(F32)[:, None] * freqs[None, :]
    c, s = jnp.cos(ang), jnp.sin(ang)
    n = pos.shape[0]
    tail = jnp.zeros((n, LANES - lead - dr), F32)
    return (jnp.concatenate([jnp.ones((n, lead), F32), c, c, tail], axis=1),
            jnp.concatenate([jnp.zeros((n, lead), F32), -s, s, tail], axis=1))


def _swap_halves(w):
    half = w.shape[-1] // 2
    return jnp.concatenate([w[..., half:], w[..., :half]], axis=-1)


def _place_lanes(w, lead):
    return jnp.pad(w, [(0, 0)] * (w.ndim - 1) + [(lead, LANES - lead - w.shape[-1])])


def kernel(x_prompt, x_sample, cache_mla_ckv, cache_mla_krope, cache_sb_k, cache_sb_v, page_table,
           p_prompt, p_sample, pre_norm_g, w_in, q_norm_g, w_uq, kv_norm_g, w_uk, w_uv,
           w_branch_a, w_branch_b, w_out, post_norm_g, ple_norm_g, w_ple_gate, w_ple):
    batch, seq, d = x_prompt.shape
    db, dec_seq, _ = x_sample.shape
    depth = w_in.shape[0]
    q_lora = q_norm_g.shape[1]
    kv_lora, heads, nope = w_uk.shape[1:]
    vdim = w_uv.shape[3]
    rope = cache_mla_krope.shape[3]
    n_phys, page, sb_heads, hd = cache_sb_k.shape[1:]
    n_pages = page_table.shape[1]
    past_len = n_pages * page
    mla_width = heads * vdim
    sb_width = sb_heads * hd
    assert sb_heads == heads and mla_width == sb_width and 2 * hd == LANES and nope + rope <= LANES
    q_scale = LOG2E / math.sqrt(nope + rope)
    sb_scale = LOG2E / math.sqrt(hd)
    dims = dict(heads=heads, kv_lora=kv_lora, rope=rope, mla_width=mla_width, sb_width=sb_width,
                sb_head_dim=hd)

    sizes = (q_lora, kv_lora, rope, mla_width, sb_width, sb_width, sb_width, sb_width, d, d)
    offs = [0]
    for n in sizes:
        offs.append(offs[-1] + n)
    names = ("cq", "ckv", "kra", "krb", "krah", "krbh", "ga", "sq", "sk", "sv", "gb", "ma", "mb")
    widths = (q_lora, kv_lora, LANES, LANES, LANES, LANES, mla_width, sb_width, sb_width, sb_width, sb_width, d, d)
    cols, lo = {}, 0
    for n, w in zip(names, widths):
        cols[n] = (lo, lo + w)
        lo += w

    pos_p = jnp.arange(seq, dtype=jnp.int32)
    pos_s = past_len + (jnp.arange(db * dec_seq, dtype=jnp.int32) % dec_seq)
    tabs_p = _rope_tables(pos_p, rope, 0) + _rope_tables(pos_p, rope, nope)
    tabs_s = _rope_tables(pos_s, rope, 0) + _rope_tables(pos_s, rope, nope)
    tri1 = -(lax.broadcasted_iota(jnp.int32, (MXU_TILE, MXU_TILE), 0)
             >= lax.broadcasted_iota(jnp.int32, (MXU_TILE, MXU_TILE), 1)).astype(BF16)
    tri = jnp.concatenate([tri1, tri1], axis=0)

    krt_c = jnp.transpose(cache_mla_krope, (0, 1, 3, 2))
    kt_c = jnp.transpose(cache_sb_k, (0, 1, 3, 4, 2)).reshape(depth, n_phys, sb_width, page)
    vt_c = jnp.transpose(cache_sb_v, (0, 1, 3, 4, 2)).reshape(depth, n_phys, sb_width, page)
    pt_flat = page_table.reshape(-1)
    head_mask = (jnp.arange(heads)[:, None] == (jnp.arange(sb_width) // hd)[None, :])

    hp = x_prompt.reshape(batch * seq, d)
    hs = x_sample.reshape(db * dec_seq, d)
    outs = [[] for _ in range(8)]
    tm_p = 256
    for i in range(depth):
        w = w_in[i]
        parts = [w[:, offs[k]:offs[k + 1]] for k in range(len(sizes))]
        cq_w, ckv_w, kr_w, ga_w, sq_w, sk_w, sv_w, gb_w, ma_w, mb_w = parts
        kr_sw = _swap_halves(kr_w)
        w_in2 = jnp.concatenate(
            [cq_w, ckv_w, _place_lanes(kr_w, 0), _place_lanes(kr_sw, 0), _place_lanes(kr_w, nope),
             _place_lanes(kr_sw, nope), ga_w, sq_w * sb_scale, sk_w, sv_w, gb_w, ma_w, mb_w], axis=1).astype(BF16)
        uq = w_uq[i].reshape(q_lora, heads, nope + rope) * q_scale
        uq_nope, uq_rope = uq[:, :, :nope], uq[:, :, nope:]
        hw = heads * LANES
        wqr_s = _place_lanes(uq_rope, 0).reshape(q_lora, hw).astype(BF16)
        wqrs_s = _place_lanes(_swap_halves(uq_rope), 0).reshape(q_lora, hw).astype(BF16)
        w_qlat = _fold_qlat(jnp.transpose(w_uq[i].reshape(q_lora, heads, nope + rope)[:, :, :nope], (1, 0, 2)),
                            jnp.transpose(w_uk[i], (1, 0, 2)), q_scale)
        wq_p = _place_lanes(jnp.concatenate([uq_nope, uq_rope], axis=-1), 0).reshape(q_lora, hw).astype(BF16)
        wqs_p = _place_lanes(_swap_halves(uq_rope), nope).reshape(q_lora, hw).astype(BF16)
        wk_p = _place_lanes(w_uk[i], 0).reshape(kv_lora, hw).astype(BF16)
        wv_p = w_uv[i].reshape(kv_lora, mla_width).astype(BF16)
        uv = jnp.transpose(w_uv[i], (1, 0, 2))
        wuv_bd = (uv[:, :, None, :] * jnp.eye(heads, dtype=F32)[:, None, :, None]).reshape(
            heads * kv_lora, mla_width).astype(BF16)
        wa, wb, wo = w_branch_a[i].astype(BF16), w_branch_b[i].astype(BF16), w_out[i].astype(BF16)
        wpg, wple = w_ple_gate[i].astype(BF16), w_ple[i].astype(BF16)
        pre_g, q_g, kv_g = pre_norm_g[i][None], q_norm_g[i][None], kv_norm_g[i][None]
        post_g, ple_g = post_norm_g[i][None], ple_norm_g[i][None]
        proj = functools.partial(_proj, pre_g=pre_g, q_g=q_g, kv_g=kv_g, w_in2=w_in2, cols=cols, dims=dims)
        merge = functools.partial(_merge, wuv_bd=wuv_bd, wa=wa, wb=wb, wo=wo, post_g=post_g, ple_g=ple_g,
                                  wpg=wpg, wple=wple)

        (ckv, krope, sga, sqb, sk, sv, sgb, sma, smb, q, k, v, skb, svb) = proj(
            hp, tabs_p, wq=wq_p, wqs=wqs_p, wk=wk_p, wv=wv_p, tm=tm_p, absorbed=False)
        va = _mla_prompt(q, k, v, batch=batch, seq=seq, dims=dims, hps=4, tq=256, tk=512)
        osb = _sb_prompt(sqb, skb, svb, tri, batch=batch, seq=seq, dims=dims, tb=MXU_TILE, npair=2)
        hp = merge(hp, va, osb, sga, sgb, sma, smb, p_prompt[i].reshape(batch * seq, -1),
                   latent_in=False, tm=tm_p)
        outs[0].append(ckv.reshape(batch, seq, kv_lora))
        outs[1].append(krope.reshape(batch, seq, rope))
        outs[2].append(sk.reshape(batch, seq, sb_heads, hd))
        outs[3].append(sv.reshape(batch, seq, sb_heads, hd))

        rows_s = db * dec_seq
        (ckv, krope, sga, sqb, sk, sv, sgb, sma, smb, q) = proj(
            hs, tabs_s, wq=wqr_s, wqs=wqrs_s, wk=w_qlat, wv=wv_p, tm=rows_s, absorbed=True)
        qm = jnp.transpose(q.reshape(heads, db, dec_seq, -1), (1, 2, 0, 3)).reshape(db, dec_seq * heads, -1)
        qb = jnp.where(head_mask[None, None], sqb.reshape(db, dec_seq, 1, sb_width),
                       jnp.zeros((), BF16)).reshape(db, dec_seq * heads, sb_width)
        olat, osb = _sample_attn(
            pt_flat, qm, qb, ckv.reshape(db, dec_seq, kv_lora), krope.reshape(db, dec_seq, rope),
            sk.reshape(db, dec_seq, sb_width), sv.reshape(db, dec_seq, sb_width), tri,
            cache_mla_ckv, krt_c, kt_c, vt_c, layer=i, dims=dims, ch=8, nslot=4)
        hs = merge(hs, olat.reshape(rows_s, heads * kv_lora), osb.reshape(rows_s, sb_width), sga, sgb, sma, smb,
                   p_sample[i].reshape(rows_s, -1), latent_in=True, tm=rows_s)
        outs[4].append(ckv.reshape(db, dec_seq, kv_lora))
        outs[5].append(krope.reshape(db, dec_seq, rope))
        outs[6].append(sk.reshape(db, dec_seq, sb_heads, hd))
        outs[7].append(sv.reshape(db, dec_seq, sb_heads, hd))

    return (hp.reshape(batch, seq, d), hs.reshape(db, dec_seq, d)) + tuple(jnp.stack(o, axis=0) for o in outs)
```

```python
import functools
import math

import jax
import jax.numpy as jnp
from jax import lax
from jax.experimental import pallas as pl
from jax.experimental.pallas import tpu as pltpu

F32 = jnp.float32
BF16 = jnp.bfloat16
EPS = 1e-6
ROPE_BASE = 10000.0
NEG_INF = -1e30
LOG2E = math.log2(math.e)
LANES = 128
MXU_TILE = 256
VMEM_LIMIT = 56 * 1024 * 1024

_NT = (((1,), (1,)), ((), ()))


def _cparams(sems):
    return pltpu.CompilerParams(dimension_semantics=sems, vmem_limit_bytes=VMEM_LIMIT)


def _rms(x, g):
    return x * lax.rsqrt(jnp.mean(x * x, axis=-1, keepdims=True) + EPS) * g


def _sigmoid(x):
    return jax.nn.sigmoid(x)


def _const_spec(shape):
    nd = len(shape)
    return pl.BlockSpec(shape, lambda *_: (0,) * nd)


def _softplus2(z2):
    return jnp.maximum(z2, 0.0) + jnp.log(1.0 + jnp.exp2(-jnp.abs(z2))) * LOG2E


def _split_bf16(x):
    hi = x.astype(BF16)
    lo = (x - hi.astype(F32)).astype(BF16)
    return hi, lo


def _fold_kernel(a_ref, b_ref, o_ref, *, scale):
    w = lax.dot_general(a_ref[0], b_ref[0], _NT, precision=lax.Precision.HIGHEST,
                        preferred_element_type=F32)
    o_ref[...] = (w * scale).astype(o_ref.dtype)


def _fold_qlat(nope_t, uk_t, scale):
    h, ql, n = nope_t.shape
    r = uk_t.shape[1]
    return pl.pallas_call(
        functools.partial(_fold_kernel, scale=scale),
        grid=(h,),
        in_specs=[pl.BlockSpec((1, ql, n), lambda i: (i, 0, 0)),
                  pl.BlockSpec((1, r, n), lambda i: (i, 0, 0))],
        out_specs=pl.BlockSpec((ql, r), lambda i: (0, i)),
        out_shape=jax.ShapeDtypeStruct((ql, h * r), BF16),
        compiler_params=_cparams(("arbitrary",)),
        name="fold_qlat",
    )(nope_t, uk_t)


def _proj_kernel(x_ref, cos0_ref, sin0_ref, cosh_ref, sinh_ref, preg_ref, qg_ref, kvg_ref, win_ref,
                 wq_ref, wqs_ref, wk_ref, wv_ref, *out_refs, cols, heads, kv_lora, rope, absorbed):
    (ckv_ref, kr_ref, sga_ref, sqb_ref, sk_ref, sv_ref, sgb_ref, sma_ref, smb_ref) = out_refs[:9]
    xn = _rms(x_ref[...], preg_ref[...]).astype(BF16)

    def proj(name):
        lo, hi = cols[name]
        return jnp.dot(xn, win_ref[:, lo:hi], preferred_element_type=F32)

    cqn = _rms(proj("cq"), qg_ref[...]).astype(BF16)
    ckvn = _rms(proj("ckv"), kvg_ref[...])
    krope = proj("kra") * cos0_ref[...] + proj("krb") * sin0_ref[...]
    ckv_ref[...] = ckvn
    kr_ref[...] = krope[:, :rope]

    if absorbed:
        (q_ref,) = out_refs[9:]
        cos, sin = cos0_ref[...], sin0_ref[...]
        qlat = jnp.dot(cqn, wk_ref[...], preferred_element_type=F32)
        qr = jnp.dot(cqn, wq_ref[...], preferred_element_type=F32)
        qrs = jnp.dot(cqn, wqs_ref[...], preferred_element_type=F32)
        for h in range(heads):
            q_ref[h, :, 0:kv_lora] = qlat[:, h * kv_lora:(h + 1) * kv_lora].astype(BF16)
            sl = slice(h * LANES, (h + 1) * LANES)
            q_ref[h, :, kv_lora:kv_lora + LANES] = (qr[:, sl] * cos + qrs[:, sl] * sin).astype(BF16)
    else:
        (q_ref, k_ref, v_ref, skb_ref, svb_ref) = out_refs[9:]
        cos, sin = cosh_ref[...], sinh_ref[...]
        ckvb = ckvn.astype(BF16)
        knope = jnp.dot(ckvb, wk_ref[...], preferred_element_type=F32)
        krh = proj("krah") * cos + proj("krbh") * sin
        qn = jnp.dot(cqn, wq_ref[...], preferred_element_type=F32)
        qs = jnp.dot(cqn, wqs_ref[...], preferred_element_type=F32)
        for h in range(heads):
            sl = slice(h * LANES, (h + 1) * LANES)
            k_ref[:, sl] = (knope[:, sl] + krh).astype(BF16)
            q_ref[:, sl] = (qn[:, sl] * cos + qs[:, sl] * sin).astype(BF16)
        v_ref[...] = jnp.dot(ckvb, wv_ref[...], preferred_element_type=F32).astype(BF16)

    ga = proj("ga")
    sga_ref[...] = (ga * _sigmoid(ga)).astype(BF16)
    sqb_ref[...] = proj("sq").astype(BF16)
    sk = proj("sk")
    sk_ref[...] = sk
    sv = proj("sv")
    sv_ref[...] = sv
    if not absorbed:
        skb_ref[...] = sk.astype(BF16)
        svb_ref[...] = sv.astype(BF16)
    gb = proj("gb")
    sgb_ref[...] = (gb * _sigmoid(gb)).astype(BF16)
    sma_ref[...] = _sigmoid(proj("ma")).astype(BF16)
    smb_ref[...] = _sigmoid(proj("mb")).astype(BF16)


def _proj(x2d, tabs, pre_g, q_g, kv_g, w_in2, wq, wqs, wk, wv, *, cols, dims, tm, absorbed):
    rows, d = x2d.shape
    heads, kv_lora, rope = dims["heads"], dims["kv_lora"], dims["rope"]
    a_w, b_w = dims["mla_width"], dims["sb_width"]
    nblk = rows // tm
    ntab = tabs[0].shape[0] // tm

    def row(n):
        return pl.BlockSpec((tm, n), lambda i: (i, 0))

    tab = pl.BlockSpec((tm, LANES), lambda i: (i % ntab, 0))
    consts = [pre_g, q_g, kv_g, w_in2, wq, wqs, wk, wv]
    in_specs = [row(d), tab, tab, tab, tab] + [_const_spec(c.shape) for c in consts]
    out_shape = [
        jax.ShapeDtypeStruct((rows, kv_lora), F32),
        jax.ShapeDtypeStruct((rows, rope), F32),
        jax.ShapeDtypeStruct((rows, a_w), BF16),
        jax.ShapeDtypeStruct((rows, b_w), BF16),
        jax.ShapeDtypeStruct((rows, b_w), F32),
        jax.ShapeDtypeStruct((rows, b_w), F32),
        jax.ShapeDtypeStruct((rows, b_w), BF16),
        jax.ShapeDtypeStruct((rows, d), BF16),
        jax.ShapeDtypeStruct((rows, d), BF16),
    ]
    out_specs = [row(kv_lora), row(rope), row(a_w), row(b_w), row(b_w), row(b_w), row(b_w), row(d), row(d)]
    if absorbed:
        qk = kv_lora + LANES
        out_shape.append(jax.ShapeDtypeStruct((heads, rows, qk), BF16))
        out_specs.append(pl.BlockSpec((heads, tm, qk), lambda i: (0, i, 0)))
    else:
        hw = heads * LANES
        out_shape += [jax.ShapeDtypeStruct((rows, hw), BF16), jax.ShapeDtypeStruct((rows, hw), BF16),
                      jax.ShapeDtypeStruct((rows, a_w), BF16), jax.ShapeDtypeStruct((rows, b_w), BF16),
                      jax.ShapeDtypeStruct((rows, b_w), BF16)]
        out_specs += [row(hw), row(hw), row(a_w), row(b_w), row(b_w)]
    return pl.pallas_call(
        functools.partial(_proj_kernel, cols=cols, heads=heads, kv_lora=kv_lora, rope=rope, absorbed=absorbed),
        grid=(nblk,),
        in_specs=in_specs,
        out_specs=out_specs,
        out_shape=out_shape,
        compiler_params=_cparams(("parallel",)),
        name="proj_sample" if absorbed else "proj_prompt",
    )(x2d, *tabs, *consts)


def _mla_kernel(q_ref, k_ref, v_ref, o_ref, m_ref, l_ref, acc_ref, *, hps, vd, tq, tk):
    i = pl.program_id(2)
    rows = hps * tq
    qs = [q_ref[:, h * LANES:(h + 1) * LANES] for h in range(hps)]
    m_ref[...] = jnp.full(m_ref.shape, NEG_INF, F32)
    l_ref[...] = jnp.zeros(l_ref.shape, F32)
    acc_ref[...] = jnp.zeros(acc_ref.shape, F32)
    nfull = (i * tq) // tk

    def scores(j):
        off = pl.multiple_of(j * tk, tk)
        return jnp.concatenate(
            [lax.dot_general(qs[h], k_ref[pl.ds(off, tk), h * LANES:(h + 1) * LANES], _NT,
                             preferred_element_type=F32) for h in range(hps)], axis=0)

    def step(j, masked, s):
        off = pl.multiple_of(j * tk, tk)
        v = v_ref[pl.ds(off, tk), :]
        m_prev = m_ref[...]
        if masked:
            qpos = i * tq + (lax.broadcasted_iota(jnp.int32, (rows, LANES), 0) & (tq - 1))
            kpos = j * tk + lax.broadcasted_iota(jnp.int32, (rows, LANES), 1)
            s = jnp.concatenate(
                [jnp.where(kpos + c * LANES <= qpos, s[:, c * LANES:(c + 1) * LANES], NEG_INF)
                 for c in range(tk // LANES)], axis=1)
        m_new = jnp.maximum(m_prev, jnp.max(s, axis=1, keepdims=True))
        alpha = jnp.exp2(m_prev - m_new)
        p = jnp.concatenate(
            [jnp.exp2(s[:, c * LANES:(c + 1) * LANES] - m_new) for c in range(tk // LANES)], axis=1)
        l_ref[...] = alpha * l_ref[...] + jnp.sum(p, axis=1, keepdims=True)
        m_ref[...] = m_new
        pb = p.astype(BF16)
        pv = jnp.concatenate(
            [jnp.dot(pb[g * 2 * tq:(g + 1) * 2 * tq], v[:, g * LANES:(g + 1) * LANES],
                     preferred_element_type=F32) for g in range(hps // 2)], axis=0)
        acc_ref[...] = alpha * acc_ref[...] + pv

    def body(j, carry):
        step(j, False, scores(j))
        return carry

    lax.fori_loop(0, nfull, body, 0)
    step(nfull, True, scores(nfull))

    o = acc_ref[...] / l_ref[...]
    lane = lax.broadcasted_iota(jnp.int32, (tq, LANES), 1)
    for g in range(hps // 2):
        lo = o[(2 * g) * tq:(2 * g + 1) * tq]
        hi = o[(2 * g + 1) * tq:(2 * g + 2) * tq]
        o_ref[:, g * LANES:(g + 1) * LANES] = jnp.where(lane < vd, lo, hi)


def _mla_prompt(q, k, v, *, batch, seq, dims, hps, tq, tk):
    heads = dims["heads"]
    vd = v.shape[-1] // heads
    assert 2 * vd == LANES and hps % 2 == 0
    nq = seq // tq
    return pl.pallas_call(
        functools.partial(_mla_kernel, hps=hps, vd=vd, tq=tq, tk=tk),
        grid=(batch, heads // hps, nq),
        in_specs=[pl.BlockSpec((tq, hps * LANES), lambda b, g, i: (b * nq + i, g)),
                  pl.BlockSpec((seq, hps * LANES), lambda b, g, i: (b, g)),
                  pl.BlockSpec((seq, hps * vd), lambda b, g, i: (b, g))],
        out_specs=pl.BlockSpec((tq, hps * vd), lambda b, g, i: (b * nq + i, g)),
        out_shape=jax.ShapeDtypeStruct((batch * seq, heads * vd), F32),
        scratch_shapes=[pltpu.VMEM((hps * tq, LANES), F32), pltpu.VMEM((hps * tq, LANES), F32),
                        pltpu.VMEM((hps * tq, LANES), F32)],
        compiler_params=_cparams(("parallel", "parallel", "arbitrary")),
        name="mla_prompt",
    )(q, k, v)


def _sb_kernel(q_ref, k_ref, v_ref, tri_ref, o_ref, acc_ref, car_ref, pend_ref, z_ref, a_ref, *, tb, hd, npair):
    i = pl.program_id(2)
    lane = lax.broadcasted_iota(jnp.int32, (tb, LANES), 1)
    qs = []
    for p in range(npair):
        q2 = q_ref[:, p * LANES:(p + 1) * LANES].astype(F32)
        qs.append(jnp.concatenate([jnp.where(lane < hd, q2, 0.0), jnp.where(lane >= hd, q2, 0.0)],
                                  axis=0).astype(BF16))
    acc_ref[...] = jnp.zeros(acc_ref.shape, F32)
    car_ref[...] = jnp.zeros(car_ref.shape, F32)
    pend_ref[...] = jnp.zeros(pend_ref.shape, F32)

    def scores(j):
        off = pl.multiple_of(j * tb, tb)
        return [lax.dot_general(qs[p], k_ref[pl.ds(off, tb), p * LANES:(p + 1) * LANES], _NT,
                                preferred_element_type=F32) for p in range(npair)]

    def flush(j):
        off = pl.multiple_of(j * tb, tb)
        acc_ref[...] += jnp.concatenate(
            [jnp.dot(a_ref[p * 2 * tb:(p + 1) * 2 * tb], v_ref[pl.ds(off, tb), p * LANES:(p + 1) * LANES],
                     preferred_element_type=F32) for p in range(npair)], axis=0)

    def step(j, masked, zs):
        tri = tri_ref[...]
        if masked:
            valid = [(lax.broadcasted_iota(jnp.int32, (2 * tb, LANES), 1) + c * LANES)
                     < (lax.broadcasted_iota(jnp.int32, (2 * tb, LANES), 0) & (tb - 1))
                     for c in range(tb // LANES)]
        pairs = range(npair)
        carry = car_ref[...] - jnp.sum(pend_ref[...], axis=1, keepdims=True)
        sps = [_softplus2(z) for z in zs]
        if masked:
            sps = [jnp.concatenate([jnp.where(valid[c], sp[:, c * LANES:(c + 1) * LANES], 0.0)
                                    for c in range(tb // LANES)], axis=1) for sp in sps]
        incls = [jnp.dot(sp.astype(BF16), tri, preferred_element_type=F32) for sp in sps]
        weights = []
        for p in pairs:
            cp = carry[p * 2 * tb:(p + 1) * 2 * tb]
            parts = []
            for c in range(tb // LANES):
                sl = slice(c * LANES, (c + 1) * LANES)
                a = jnp.exp2(zs[p][:, sl] + incls[p][:, sl] + cp)
                if masked:
                    a = jnp.where(valid[c], a, 0.0)
                parts.append(a)
            a = jnp.concatenate(parts, axis=1).astype(BF16)
            weights.append(a)
        a_ref[...] = jnp.concatenate(weights, axis=0)
        car_ref[...] = carry
        pend_ref[...] = jnp.concatenate(
            [functools.reduce(lambda x, y: x + y, [sp[:, c * LANES:(c + 1) * LANES] for c in range(tb // LANES)])
             for sp in sps], axis=0)

    z_diag = scores(i)
    z_ref[...] = jnp.concatenate(scores(jnp.maximum(i - 1, 0)), axis=0)
    step(i, True, z_diag)

    def body(t, carry):
        j = i - 1 - t
        z_all = z_ref[...]
        z_next = jnp.concatenate(scores(jnp.maximum(j - 1, 0)), axis=0)
        flush(j + 1)
        step(j, False, [z_all[p * 2 * tb:(p + 1) * 2 * tb] for p in range(npair)])
        z_ref[...] = z_next
        return carry

    lax.fori_loop(0, i, body, 0)
    flush(0)
    for p in range(npair):
        o_ref[:, p * LANES:(p + 1) * LANES] = jnp.where(
            lane < hd, acc_ref[(2 * p) * tb:(2 * p + 1) * tb], acc_ref[(2 * p + 1) * tb:(2 * p + 2) * tb])


def _sb_prompt(sqb, skb, svb, tri, *, batch, seq, dims, tb, npair):
    hd = dims["sb_head_dim"]
    width = sqb.shape[-1]
    bw = npair * 2 * hd
    nq = seq // tb
    return pl.pallas_call(
        functools.partial(_sb_kernel, tb=tb, hd=hd, npair=npair),
        grid=(batch, width // bw, nq),
        in_specs=[pl.BlockSpec((tb, bw), lambda b, g, i: (b * nq + i, g)),
                  pl.BlockSpec((seq, bw), lambda b, g, i: (b, g)),
                  pl.BlockSpec((seq, bw), lambda b, g, i: (b, g)),
                  _const_spec(tri.shape)],
        out_specs=pl.BlockSpec((tb, bw), lambda b, g, i: (b * nq + i, g)),
        out_shape=jax.ShapeDtypeStruct((batch * seq, width), F32),
        scratch_shapes=[pltpu.VMEM((npair * 2 * tb, 2 * hd), F32), pltpu.VMEM((npair * 2 * tb, LANES), F32),
                        pltpu.VMEM((npair * 2 * tb, LANES), F32), pltpu.VMEM((npair * 2 * tb, tb), F32),
                        pltpu.VMEM((npair * 2 * tb, tb), BF16)],
        compiler_params=_cparams(("parallel", "parallel", "arbitrary")),
        name="sb_prompt",
    )(sqb, skb, svb, tri)


def _sample_kernel(pt_ref, qm_ref, qb_ref, ckvn_ref, krn_ref, skn_ref, svn_ref, tri_ref,
                   ckv_hbm, krt_hbm, kt_hbm, vt_hbm,
                   olat_ref, osb_ref,
                   ckv_buf, krt_buf, kt_buf, vt_buf, sem, m_ref, l_ref, accl_ref, accs_ref, car_ref,
                   *, layer, n_pages, ch, page, nslot, heads, kv_lora, rope, hd, dec_seq):
    b = pl.program_id(0)
    nb = pl.num_programs(0)
    nch = n_pages // ch
    ahead = nslot - 1
    rows = dec_seq * heads
    keys = ch * page
    nsub = keys // MXU_TILE

    def copies(bb, k, slot):
        c = nch - 1 - k
        out = []
        for j in range(ch):
            pg = pt_ref[bb * n_pages + c * ch + j]
            ks = pl.ds(j * page, page)
            out.append(pltpu.make_async_copy(ckv_hbm.at[layer, pg], ckv_buf.at[slot, ks, :], sem.at[slot, 0]))
            out.append(pltpu.make_async_copy(krt_hbm.at[layer, pg], krt_buf.at[slot, :, ks], sem.at[slot, 1]))
            out.append(pltpu.make_async_copy(kt_hbm.at[layer, pg], kt_buf.at[slot, :, ks], sem.at[slot, 2]))
            out.append(pltpu.make_async_copy(vt_hbm.at[layer, pg], vt_buf.at[slot, :, ks], sem.at[slot, 3]))
        return out

    def start(bb, k, slot):
        for cp in copies(bb, k, slot):
            cp.start()

    def wait(bb, k, slot):
        for cp in copies(bb, k, slot):
            cp.wait()

    @pl.when(b == 0)
    def _():
        for k in range(ahead):
            start(0, k, k)

    qm = qm_ref[0]
    qlat = qm[:, 0:kv_lora]
    qrope = qm[:, kv_lora:kv_lora + rope]
    qb = qb_ref[0]
    tok = lax.broadcasted_iota(jnp.int32, (rows, 1), 0) // heads

    qlat_f = qlat.astype(F32)
    qrope_f = qrope.astype(F32)
    qb_f = qb.astype(F32)
    ckvn = ckvn_ref[0]
    krn = krn_ref[0]
    skn = skn_ref[0]
    svn = svn_ref[0]
    m = jnp.full((rows, 1), NEG_INF, F32)
    l = jnp.zeros((rows, 1), F32)
    accl = jnp.zeros((rows, kv_lora), F32)
    for s in range(dec_seq):
        sc = (jnp.sum(qlat_f * ckvn[s:s + 1, :], axis=1, keepdims=True)
              + jnp.sum(qrope_f * krn[s:s + 1, :], axis=1, keepdims=True))
        sc = jnp.where(tok >= s, sc, NEG_INF)
        m_new = jnp.maximum(m, sc)
        alpha = jnp.exp2(m - m_new)
        p = jnp.exp2(sc - m_new)
        l = alpha * l + p
        accl = alpha * accl + p * ckvn[s:s + 1, :]
        m = m_new
    carry = jnp.zeros((rows, 1), F32)
    accs = jnp.zeros((rows, heads * hd), F32)
    for s in reversed(range(dec_seq)):
        z = jnp.sum(qb_f * skn[s:s + 1, :], axis=1, keepdims=True)
        sp = _softplus2(z)
        valid = tok > s
        a = jnp.where(valid, jnp.exp2(z - sp + carry), 0.0)
        accs = accs + a * svn[s:s + 1, :]
        carry = carry - jnp.where(valid, sp, 0.0)
    m_ref[...] = m
    l_ref[...] = l
    accl_ref[...] = accl
    accs_ref[...] = accs
    car_ref[...] = carry

    def compute(slot):
        kc = ckv_buf[slot].astype(BF16)
        krt = krt_buf[slot].astype(BF16)
        s = (lax.dot_general(qlat, kc, _NT, preferred_element_type=F32)
             + jnp.dot(qrope, krt, preferred_element_type=F32))
        m_prev = m_ref[...]
        m_new = jnp.maximum(m_prev, jnp.max(s, axis=1, keepdims=True))
        alpha = jnp.exp2(m_prev - m_new)
        p = jnp.exp2(s - m_new)
        l_ref[...] = alpha * l_ref[...] + jnp.sum(p, axis=1, keepdims=True)
        m_ref[...] = m_new
        accl_ref[...] = alpha * accl_ref[...] + jnp.dot(p.astype(BF16), kc, preferred_element_type=F32)

        kt = kt_buf[slot].astype(BF16)
        vt = vt_buf[slot].astype(BF16)
        z = jnp.dot(qb, kt, preferred_element_type=F32)
        sp = _softplus2(z)
        hi, lo = _split_bf16(sp)
        stacked = jnp.concatenate(
            [jnp.concatenate([hi[:, u * MXU_TILE:(u + 1) * MXU_TILE], lo[:, u * MXU_TILE:(u + 1) * MXU_TILE]],
                             axis=1) for u in range(nsub)], axis=0)
        incl = jnp.dot(stacked, tri_ref[...], preferred_element_type=F32)
        car = car_ref[...]
        parts = [None] * nsub
        for u in reversed(range(nsub)):
            sl = slice(u * MXU_TILE, (u + 1) * MXU_TILE)
            parts[u] = jnp.exp2(z[:, sl] + incl[u * rows:(u + 1) * rows] + car)
            car = car - jnp.sum(sp[:, sl], axis=1, keepdims=True)
        car_ref[...] = car
        a = jnp.concatenate(parts, axis=1).astype(BF16)
        accs_ref[...] += lax.dot_general(a, vt, _NT, preferred_element_type=F32)

    def group(it, carry_):
        for u in range(nslot):
            k = it * nslot + u
            kn = k + ahead
            bn = b + kn // nch

            @pl.when(bn < nb)
            def _():
                start(bn, kn % nch, (u + ahead) % nslot)

            wait(b, k, u)
            compute(u)
        return carry_

    lax.fori_loop(0, nch // nslot, group, 0)

    olat_ref[0] = accl_ref[...] / l_ref[...]
    accs = accs_ref[...]
    rid = lax.broadcasted_iota(jnp.int32, accs.shape, 0) % heads
    hid = lax.broadcasted_iota(jnp.int32, accs.shape, 1) // hd
    diag = jnp.where(rid == hid, accs, 0.0)
    osb_ref[0] = jnp.sum(diag.reshape(dec_seq, heads, heads * hd), axis=1)


def _sample_attn(pt_flat, qm, qb, ckvn, krn, skn, svn, tri, ckv_c, krt_c, kt_c, vt_c, *, layer, dims, ch, nslot):
    db, rows, qk = qm.shape
    heads, kv_lora, rope, hd = dims["heads"], dims["kv_lora"], dims["rope"], dims["sb_head_dim"]
    dec_seq = rows // heads
    page = ckv_c.shape[2]
    n_pages = pt_flat.shape[0] // db
    assert n_pages % (ch * nslot) == 0
    keys = ch * page
    width = heads * hd

    def per_b(shape):
        nd = len(shape)
        return pl.BlockSpec((1,) + tuple(shape[1:]), lambda b, pt: (b,) + (0,) * (nd - 1))

    anyspec = pl.BlockSpec(memory_space=pl.ANY)
    grid_spec = pltpu.PrefetchScalarGridSpec(
        num_scalar_prefetch=1,
        grid=(db,),
        in_specs=[per_b(qm.shape), per_b(qb.shape), per_b(ckvn.shape), per_b(krn.shape), per_b(skn.shape),
                  per_b(svn.shape), pl.BlockSpec(tri.shape, lambda b, pt: (0, 0)),
                  anyspec, anyspec, anyspec, anyspec],
        out_specs=[pl.BlockSpec((1, rows, kv_lora), lambda b, pt: (b, 0, 0)),
                   pl.BlockSpec((1, dec_seq, width), lambda b, pt: (b, 0, 0))],
        scratch_shapes=[pltpu.VMEM((nslot, keys, kv_lora), F32), pltpu.VMEM((nslot, rope, keys), F32),
                        pltpu.VMEM((nslot, width, keys), F32), pltpu.VMEM((nslot, width, keys), F32),
                        pltpu.SemaphoreType.DMA((nslot, 4)),
                        pltpu.VMEM((rows, 1), F32), pltpu.VMEM((rows, 1), F32),
                        pltpu.VMEM((rows, kv_lora), F32), pltpu.VMEM((rows, width), F32),
                        pltpu.VMEM((rows, 1), F32)],
    )
    return pl.pallas_call(
        functools.partial(_sample_kernel, layer=layer, n_pages=n_pages, ch=ch, page=page, nslot=nslot,
                          heads=heads, kv_lora=kv_lora, rope=rope, hd=hd, dec_seq=dec_seq),
        grid_spec=grid_spec,
        out_shape=[jax.ShapeDtypeStruct((db, rows, kv_lora), F32),
                   jax.ShapeDtypeStruct((db, dec_seq, width), F32)],
        compiler_params=_cparams(("arbitrary",)),
        name="sample_attn",
    )(pt_flat, qm, qb, ckvn, krn, skn, svn, tri, ckv_c, krt_c, kt_c, vt_c)


def _merge_kernel(x_ref, va_ref, osb_ref, sga_ref, sgb_ref, sma_ref, smb_ref, p_ref, wuv_ref, wa_ref, wb_ref,
                  wo_ref, postg_ref, pleg_ref, wpg_ref, wple_ref, y_ref, *, latent_in):
    va = va_ref[...]
    if latent_in:
        va = jnp.dot(va.astype(BF16), wuv_ref[...], preferred_element_type=F32)
    a = (va * sga_ref[...].astype(F32)).astype(BF16)
    bb = (osb_ref[...] * sgb_ref[...].astype(F32)).astype(BF16)
    ya = jnp.dot(a, wa_ref[...], preferred_element_type=F32)
    yb = jnp.dot(bb, wb_ref[...], preferred_element_type=F32)
    mix = (sma_ref[...].astype(F32) * ya + smb_ref[...].astype(F32) * yb).astype(BF16)
    mixed = jnp.dot(mix, wo_ref[...], preferred_element_type=F32)
    h = x_ref[...] + _rms(mixed, postg_ref[...])
    gate = _sigmoid(jnp.dot(_rms(h, pleg_ref[...]).astype(BF16), wpg_ref[...], preferred_element_type=F32))
    emb = jnp.dot(p_ref[...].astype(BF16), wple_ref[...], preferred_element_type=F32)
    y_ref[...] = h + gate * emb


def _merge(x2d, va, osb, sga, sgb, sma, smb, p2d, wuv_bd, wa, wb, wo, post_g, ple_g, wpg, wple, *,
           latent_in, tm):
    rows, d = x2d.shape

    def row(arr):
        return pl.BlockSpec((tm, arr.shape[-1]), lambda i: (i, 0))

    ins = [x2d, va, osb, sga, sgb, sma, smb, p2d]
    consts = [wuv_bd, wa, wb, wo, post_g, ple_g, wpg, wple]
    return pl.pallas_call(
        functools.partial(_merge_kernel, latent_in=latent_in),
        grid=(rows // tm,),
        in_specs=[row(a) for a in ins] + [_const_spec(c.shape) for c in consts],
        out_specs=pl.BlockSpec((tm, d), lambda i: (i, 0)),
        out_shape=jax.ShapeDtypeStruct((rows, d), F32),
        compiler_params=_cparams(("parallel",)),
        name="merge",
    )(*ins, *consts)


def _rope_tables(pos, dr, lead):
    freqs = ROPE_BASE ** (-jnp.arange(0, dr, 2, dtype=F32) / dr)
    ang = pos.astype(F32)[:, None] * freqs[None, :]
    c, s = jnp.cos(ang), jnp.sin(ang)
    n = pos.shape[0]
    tail = jnp.zeros((n, LANES - lead - dr), F32)
    return (jnp.concatenate([jnp.ones((n, lead), F32), c, c, tail], axis=1),
            jnp.concatenate([jnp.zeros((n, lead), F32), -s, s, tail], axis=1))


def _swap_halves(w):
    half = w.shape[-1] // 2
    return jnp.concatenate([w[..., half:], w[..., :half]], axis=-1)


def _place_lanes(w, lead):
    return jnp.pad(w, [(0, 0)] * (w.ndim - 1) + [(lead, LANES - lead - w.shape[-1])])


def kernel(x_prompt, x_sample, cache_mla_ckv, cache_mla_krope, cache_sb_k, cache_sb_v, page_table,
           p_prompt, p_sample, pre_norm_g, w_in, q_norm_g, w_uq, kv_norm_g, w_uk, w_uv,
           w_branch_a, w_branch_b, w_out, post_norm_g, ple_norm_g, w_ple_gate, w_ple):
    batch, seq, d = x_prompt.shape
    db, dec_seq, _ = x_sample.shape
    depth = w_in.shape[0]
    q_lora = q_norm_g.shape[1]
    kv_lora, heads, nope = w_uk.shape[1:]
    vdim = w_uv.shape[3]
    rope = cache_mla_krope.shape[3]
    n_phys, page, sb_heads, hd = cache_sb_k.shape[1:]
    n_pages = page_table.shape[1]
    past_len = n_pages * page
    mla_width = heads * vdim
    sb_width = sb_heads * hd
    assert sb_heads == heads and mla_width == sb_width and 2 * hd == LANES and nope + rope <= LANES
    q_scale = LOG2E / math.sqrt(nope + rope)
    sb_scale = LOG2E / math.sqrt(hd)
    dims = dict(heads=heads, kv_lora=kv_lora, rope=rope, mla_width=mla_width, sb_width=sb_width,
                sb_head_dim=hd)

    sizes = (q_lora, kv_lora, rope, mla_width, sb_width, sb_width, sb_width, sb_width, d, d)
    offs = [0]
    for n in sizes:
        offs.append(offs[-1] + n)
    names = ("cq", "ckv", "kra", "krb", "krah", "krbh", "ga", "sq", "sk", "sv", "gb", "ma", "mb")
    widths = (q_lora, kv_lora, LANES, LANES, LANES, LANES, mla_width, sb_width, sb_width, sb_width, sb_width, d, d)
    cols, lo = {}, 0
    for n, w in zip(names, widths):
        cols[n] = (lo, lo + w)
        lo += w

    pos_p = jnp.arange(seq, dtype=jnp.int32)
    pos_s = past_len + (jnp.arange(db * dec_seq, dtype=jnp.int32) % dec_seq)
    tabs_p = _rope_tables(pos_p, rope, 0) + _rope_tables(pos_p, rope, nope)
    tabs_s = _rope_tables(pos_s, rope, 0) + _rope_tables(pos_s, rope, nope)
    tri1 = -(lax.broadcasted_iota(jnp.int32, (MXU_TILE, MXU_TILE), 0)
             >= lax.broadcasted_iota(jnp.int32, (MXU_TILE, MXU_TILE), 1)).astype(BF16)
    tri = jnp.concatenate([tri1, tri1], axis=0)

    krt_c = jnp.transpose(cache_mla_krope, (0, 1, 3, 2))
    kt_c = jnp.transpose(cache_sb_k, (0, 1, 3, 4, 2)).reshape(depth, n_phys, sb_width, page)
    vt_c = jnp.transpose(cache_sb_v, (0, 1, 3, 4, 2)).reshape(depth, n_phys, sb_width, page)
    pt_flat = page_table.reshape(-1)
    head_mask = (jnp.arange(heads)[:, None] == (jnp.arange(sb_width) // hd)[None, :])

    hp = x_prompt.reshape(batch * seq, d)
    hs = x_sample.reshape(db * dec_seq, d)
    outs = [[] for _ in range(8)]
    tm_p = 256
    for i in range(depth):
        w = w_in[i]
        parts = [w[:, offs[k]:offs[k + 1]] for k in range(len(sizes))]
        cq_w, ckv_w, kr_w, ga_w, sq_w, sk_w, sv_w, gb_w, ma_w, mb_w = parts
        kr_sw = _swap_halves(kr_w)
        w_in2 = jnp.concatenate(
            [cq_w, ckv_w, _place_lanes(kr_w, 0), _place_lanes(kr_sw, 0), _place_lanes(kr_w, nope),
             _place_lanes(kr_sw, nope), ga_w, sq_w * sb_scale, sk_w, sv_w, gb_w, ma_w, mb_w], axis=1).astype(BF16)
        uq = w_uq[i].reshape(q_lora, heads, nope + rope) * q_scale
        uq_nope, uq_rope = uq[:, :, :nope], uq[:, :, nope:]
        hw = heads * LANES
        wqr_s = _place_lanes(uq_rope, 0).reshape(q_lora, hw).astype(BF16)
        wqrs_s = _place_lanes(_swap_halves(uq_rope), 0).reshape(q_lora, hw).astype(BF16)
        w_qlat = _fold_qlat(jnp.transpose(w_uq[i].reshape(q_lora, heads, nope + rope)[:, :, :nope], (1, 0, 2)),
                            jnp.transpose(w_uk[i], (1, 0, 2)), q_scale)
        wq_p = _place_lanes(jnp.concatenate([uq_nope, uq_rope], axis=-1), 0).reshape(q_lora, hw).astype(BF16)
        wqs_p = _place_lanes(_swap_halves(uq_rope), nope).reshape(q_lora, hw).astype(BF16)
        wk_p = _place_lanes(w_uk[i], 0).reshape(kv_lora, hw).astype(BF16)
        wv_p = w_uv[i].reshape(kv_lora, mla_width).astype(BF16)
        uv = jnp.transpose(w_uv[i], (1, 0, 2))
        wuv_bd = (uv[:, :, None, :] * jnp.eye(heads, dtype=F32)[:, None, :, None]).reshape(
            heads * kv_lora, mla_width).astype(BF16)
        wa, wb, wo = w_branch_a[i].astype(BF16), w_branch_b[i].astype(BF16), w_out[i].astype(BF16)
        wpg, wple = w_ple_gate[i].astype(BF16), w_ple[i].astype(BF16)
        pre_g, q_g, kv_g = pre_norm_g[i][None], q_norm_g[i][None], kv_norm_g[i][None]
        post_g, ple_g = post_norm_g[i][None], ple_norm_g[i][None]
        proj = functools.partial(_proj, pre_g=pre_g, q_g=q_g, kv_g=kv_g, w_in2=w_in2, cols=cols, dims=dims)
        merge = functools.partial(_merge, wuv_bd=wuv_bd, wa=wa, wb=wb, wo=wo, post_g=post_g, ple_g=ple_g,
                                  wpg=wpg, wple=wple)

        (ckv, krope, sga, sqb, sk, sv, sgb, sma, smb, q, k, v, skb, svb) = proj(
            hp, tabs_p, wq=wq_p, wqs=wqs_p, wk=wk_p, wv=wv_p, tm=tm_p, absorbed=False)
        va = _mla_prompt(q, k, v, batch=batch, seq=seq, dims=dims, hps=4, tq=256, tk=512)
        osb = _sb_prompt(sqb, skb, svb, tri1, batch=batch, seq=seq, dims=dims, tb=MXU_TILE, npair=2)
        hp = merge(hp, va, osb, sga, sgb, sma, smb, p_prompt[i].reshape(batch * seq, -1),
                   latent_in=False, tm=tm_p)
        outs[0].append(ckv.reshape(batch, seq, kv_lora))
        outs[1].append(krope.reshape(batch, seq, rope))
        outs[2].append(sk.reshape(batch, seq, sb_heads, hd))
        outs[3].append(sv.reshape(batch, seq, sb_heads, hd))

        rows_s = db * dec_seq
        (ckv, krope, sga, sqb, sk, sv, sgb, sma, smb, q) = proj(
            hs, tabs_s, wq=wqr_s, wqs=wqrs_s, wk=w_qlat, wv=wv_p, tm=rows_s, absorbed=True)
        qm = jnp.transpose(q.reshape(heads, db, dec_seq, -1), (1, 2, 0, 3)).reshape(db, dec_seq * heads, -1)
        qb = jnp.where(head_mask[None, None], sqb.reshape(db, dec_seq, 1, sb_width),
                       jnp.zeros((), BF16)).reshape(db, dec_seq * heads, sb_width)
        olat, osb = _sample_attn(
            pt_flat, qm, qb, ckv.reshape(db, dec_seq, kv_lora), krope.reshape(db, dec_seq, rope),
            sk.reshape(db, dec_seq, sb_width), sv.reshape(db, dec_seq, sb_width), tri,
            cache_mla_ckv, krt_c, kt_c, vt_c, layer=i, dims=dims, ch=8, nslot=4)
        hs = merge(hs, olat.reshape(rows_s, heads * kv_lora), osb.reshape(rows_s, sb_width), sga, sgb, sma, smb,
                   p_sample[i].reshape(rows_s, -1), latent_in=True, tm=rows_s)
        outs[4].append(ckv.reshape(db, dec_seq, kv_lora))
        outs[5].append(krope.reshape(db, dec_seq, rope))
        outs[6].append(sk.reshape(db, dec_seq, sb_heads, hd))
        outs[7].append(sv.reshape(db, dec_seq, sb_heads, hd))

    return (hp.reshape(batch, seq, d), hs.reshape(db, dec_seq, d)) + tuple(jnp.stack(o, axis=0) for o in outs)
```

```python
import functools
import math

import jax
import jax.numpy as jnp
from jax import lax
from jax.experimental import pallas as pl
from jax.experimental.pallas import tpu as pltpu

F32 = jnp.float32
BF16 = jnp.bfloat16
EPS = 1e-6
ROPE_BASE = 10000.0
NEG_INF = -1e30
LOG2E = math.log2(math.e)
LANES = 128
MXU_TILE = 256
VMEM_LIMIT = 56 * 1024 * 1024

_NT = (((1,), (1,)), ((), ()))


def _cparams(sems):
    return pltpu.CompilerParams(dimension_semantics=sems, vmem_limit_bytes=VMEM_LIMIT)


def _rms(x, g):
    return x * lax.rsqrt(jnp.mean(x * x, axis=-1, keepdims=True) + EPS) * g


def _sigmoid(x):
    return jax.nn.sigmoid(x)


def _const_spec(shape):
    nd = len(shape)
    return pl.BlockSpec(shape, lambda *_: (0,) * nd)


def _softplus2(z2):
    return jnp.maximum(z2, 0.0) + jnp.log(1.0 + jnp.exp2(-jnp.abs(z2))) * LOG2E


def _split_bf16(x):
    hi = x.astype(BF16)
    lo = (x - hi.astype(F32)).astype(BF16)
    return hi, lo


def _fold_kernel(a_ref, b_ref, o_ref, *, scale):
    w = lax.dot_general(a_ref[0], b_ref[0], _NT, precision=lax.Precision.HIGHEST,
                        preferred_element_type=F32)
    o_ref[...] = (w * scale).astype(o_ref.dtype)


def _fold_qlat(nope_t, uk_t, scale):
    h, ql, n = nope_t.shape
    r = uk_t.shape[1]
    return pl.pallas_call(
        functools.partial(_fold_kernel, scale=scale),
        grid=(h,),
        in_specs=[pl.BlockSpec((1, ql, n), lambda i: (i, 0, 0)),
                  pl.BlockSpec((1, r, n), lambda i: (i, 0, 0))],
        out_specs=pl.BlockSpec((ql, r), lambda i: (0, i)),
        out_shape=jax.ShapeDtypeStruct((ql, h * r), BF16),
        compiler_params=_cparams(("arbitrary",)),
        name="fold_qlat",
    )(nope_t, uk_t)


def _proj_kernel(x_ref, cos0_ref, sin0_ref, cosh_ref, sinh_ref, preg_ref, qg_ref, kvg_ref, win_ref,
                 wq_ref, wqs_ref, wk_ref, wv_ref, wkvt_ref, *out_refs, cols, heads, kv_lora, rope, absorbed):
    (ckv_ref, kr_ref, sga_ref, sqb_ref, sk_ref, sv_ref, sgb_ref, sma_ref, smb_ref) = out_refs[:9]
    xn = _rms(x_ref[...], preg_ref[...]).astype(BF16)

    def proj(name):
        lo, hi = cols[name]
        return jnp.dot(xn, win_ref[:, lo:hi], preferred_element_type=F32)

    cqn = _rms(proj("cq"), qg_ref[...]).astype(BF16)
    ckvn = _rms(proj("ckv"), kvg_ref[...])
    krope = proj("kra") * cos0_ref[...] + proj("krb") * sin0_ref[...]
    ckv_ref[...] = ckvn
    kr_ref[...] = krope[:, :rope]

    if absorbed:
        (q_ref,) = out_refs[9:]
        cos, sin = cos0_ref[...], sin0_ref[...]
        qlat = jnp.dot(cqn, wk_ref[...], preferred_element_type=F32)
        qr = jnp.dot(cqn, wq_ref[...], preferred_element_type=F32)
        qrs = jnp.dot(cqn, wqs_ref[...], preferred_element_type=F32)
        for h in range(heads):
            q_ref[h, :, 0:kv_lora] = qlat[:, h * kv_lora:(h + 1) * kv_lora].astype(BF16)
            sl = slice(h * LANES, (h + 1) * LANES)
            q_ref[h, :, kv_lora:kv_lora + LANES] = (qr[:, sl] * cos + qrs[:, sl] * sin).astype(BF16)
    else:
        (q_ref, k_ref, v_ref, skb_ref, svb_ref) = out_refs[9:]
        cos, sin = cosh_ref[...], sinh_ref[...]
        ckvb = ckvn.astype(BF16)
        knope = jnp.dot(ckvb, wk_ref[...], preferred_element_type=F32)
        krh = proj("krah") * cos + proj("krbh") * sin
        qn = jnp.dot(cqn, wq_ref[...], preferred_element_type=F32)
        qs = jnp.dot(cqn, wqs_ref[...], preferred_element_type=F32)
        for h in range(heads):
            sl = slice(h * LANES, (h + 1) * LANES)
            k_ref[:, sl] = (knope[:, sl] + krh).astype(BF16)
            q_ref[:, sl] = (qn[:, sl] * cos + qs[:, sl] * sin).astype(BF16)
        v_ref[...] = jnp.dot(ckvb, wv_ref[...], preferred_element_type=F32).astype(BF16)

    ga = proj("ga")
    sga_ref[...] = (ga * _sigmoid(ga)).astype(BF16)
    sqb_ref[...] = proj("sq").astype(BF16)
    sk = proj("sk")
    sv = proj("sv")
    if absorbed:
        sk_ref[...] = sk
        sv_ref[...] = sv
    else:
        skb_ref[...] = sk.astype(BF16)
        svb_ref[...] = sv.astype(BF16)
        kvt = lax.dot_general(wkvt_ref[...], xn, _NT, preferred_element_type=F32)
        sk_ref[0] = kvt[:sk.shape[1]]
        sv_ref[0] = kvt[sk.shape[1]:]
    gb = proj("gb")
    sgb_ref[...] = (gb * _sigmoid(gb)).astype(BF16)
    sma_ref[...] = _sigmoid(proj("ma")).astype(BF16)
    smb_ref[...] = _sigmoid(proj("mb")).astype(BF16)


def _proj(x2d, tabs, pre_g, q_g, kv_g, w_in2, wq, wqs, wk, wv, wkvt, *, cols, dims, tm, absorbed, seq=None):
    rows, d = x2d.shape
    heads, kv_lora, rope = dims["heads"], dims["kv_lora"], dims["rope"]
    a_w, b_w = dims["mla_width"], dims["sb_width"]
    nblk = rows // tm
    ntab = tabs[0].shape[0] // tm

    def row(n):
        return pl.BlockSpec((tm, n), lambda i: (i, 0))

    tab = pl.BlockSpec((tm, LANES), lambda i: (i % ntab, 0))
    consts = [pre_g, q_g, kv_g, w_in2, wq, wqs, wk, wv, wkvt]
    in_specs = [row(d), tab, tab, tab, tab] + [_const_spec(c.shape) for c in consts]
    out_shape = [
        jax.ShapeDtypeStruct((rows, kv_lora), F32),
        jax.ShapeDtypeStruct((rows, rope), F32),
        jax.ShapeDtypeStruct((rows, a_w), BF16),
        jax.ShapeDtypeStruct((rows, b_w), BF16),
        jax.ShapeDtypeStruct((rows, b_w), F32),
        jax.ShapeDtypeStruct((rows, b_w), F32),
        jax.ShapeDtypeStruct((rows, b_w), BF16),
        jax.ShapeDtypeStruct((rows, d), BF16),
        jax.ShapeDtypeStruct((rows, d), BF16),
    ]
    out_specs = [row(kv_lora), row(rope), row(a_w), row(b_w), row(b_w), row(b_w), row(b_w), row(d), row(d)]
    if not absorbed:
        nt = seq // tm
        for o in (4, 5):
            out_shape[o] = jax.ShapeDtypeStruct((rows // seq, b_w, seq), F32)
            out_specs[o] = pl.BlockSpec((1, b_w, tm), lambda i: (i // nt, 0, i % nt))
    if absorbed:
        qk = kv_lora + LANES
        out_shape.append(jax.ShapeDtypeStruct((heads, rows, qk), BF16))
        out_specs.append(pl.BlockSpec((heads, tm, qk), lambda i: (0, i, 0)))
    else:
        hw = heads * LANES
        out_shape += [jax.ShapeDtypeStruct((rows, hw), BF16), jax.ShapeDtypeStruct((rows, hw), BF16),
                      jax.ShapeDtypeStruct((rows, a_w), BF16), jax.ShapeDtypeStruct((rows, b_w), BF16),
                      jax.ShapeDtypeStruct((rows, b_w), BF16)]
        out_specs += [row(hw), row(hw), row(a_w), row(b_w), row(b_w)]
    return pl.pallas_call(
        functools.partial(_proj_kernel, cols=cols, heads=heads, kv_lora=kv_lora, rope=rope, absorbed=absorbed),
        grid=(nblk,),
        in_specs=in_specs,
        out_specs=out_specs,
        out_shape=out_shape,
        compiler_params=_cparams(("parallel",)),
        name="proj_sample" if absorbed else "proj_prompt",
    )(x2d, *tabs, *consts)


def _mla_kernel(q_ref, k_ref, v_ref, o_ref, m_ref, l_ref, acc_ref, *, hps, vd, tq, tk):
    i = pl.program_id(2)
    rows = hps * tq
    qs = [q_ref[:, h * LANES:(h + 1) * LANES] for h in range(hps)]
    m_ref[...] = jnp.full(m_ref.shape, NEG_INF, F32)
    l_ref[...] = jnp.zeros(l_ref.shape, F32)
    acc_ref[...] = jnp.zeros(acc_ref.shape, F32)
    nfull = (i * tq) // tk

    def scores(j):
        off = pl.multiple_of(j * tk, tk)
        return jnp.concatenate(
            [lax.dot_general(qs[h], k_ref[pl.ds(off, tk), h * LANES:(h + 1) * LANES], _NT,
                             preferred_element_type=F32) for h in range(hps)], axis=0)

    def step(j, masked, s):
        off = pl.multiple_of(j * tk, tk)
        v = v_ref[pl.ds(off, tk), :]
        m_prev = m_ref[...]
        if masked:
            qpos = i * tq + (lax.broadcasted_iota(jnp.int32, (rows, LANES), 0) & (tq - 1))
            kpos = j * tk + lax.broadcasted_iota(jnp.int32, (rows, LANES), 1)
            s = jnp.concatenate(
                [jnp.where(kpos + c * LANES <= qpos, s[:, c * LANES:(c + 1) * LANES], NEG_INF)
                 for c in range(tk // LANES)], axis=1)
        m_new = jnp.maximum(m_prev, jnp.max(s, axis=1, keepdims=True))
        alpha = jnp.exp2(m_prev - m_new)
        p = jnp.concatenate(
            [jnp.exp2(s[:, c * LANES:(c + 1) * LANES] - m_new) for c in range(tk // LANES)], axis=1)
        l_ref[...] = alpha * l_ref[...] + jnp.sum(p, axis=1, keepdims=True)
        m_ref[...] = m_new
        pb = p.astype(BF16)
        pv = jnp.concatenate(
            [jnp.dot(pb[g * 2 * tq:(g + 1) * 2 * tq], v[:, g * LANES:(g + 1) * LANES],
                     preferred_element_type=F32) for g in range(hps // 2)], axis=0)
        acc_ref[...] = alpha * acc_ref[...] + pv

    def body(j, carry):
        step(j, False, scores(j))
        return carry

    lax.fori_loop(0, nfull, body, 0)
    step(nfull, True, scores(nfull))

    o = acc_ref[...] / l_ref[...]
    lane = lax.broadcasted_iota(jnp.int32, (tq, LANES), 1)
    for g in range(hps // 2):
        lo = o[(2 * g) * tq:(2 * g + 1) * tq]
        hi = o[(2 * g + 1) * tq:(2 * g + 2) * tq]
        o_ref[:, g * LANES:(g + 1) * LANES] = jnp.where(lane < vd, lo, hi)


def _mla_prompt(q, k, v, *, batch, seq, dims, hps, tq, tk):
    heads = dims["heads"]
    vd = v.shape[-1] // heads
    assert 2 * vd == LANES and hps % 2 == 0
    nq = seq // tq
    return pl.pallas_call(
        functools.partial(_mla_kernel, hps=hps, vd=vd, tq=tq, tk=tk),
        grid=(batch, heads // hps, nq),
        in_specs=[pl.BlockSpec((tq, hps * LANES), lambda b, g, i: (b * nq + i, g)),
                  pl.BlockSpec((seq, hps * LANES), lambda b, g, i: (b, g)),
                  pl.BlockSpec((seq, hps * vd), lambda b, g, i: (b, g))],
        out_specs=pl.BlockSpec((tq, hps * vd), lambda b, g, i: (b * nq + i, g)),
        out_shape=jax.ShapeDtypeStruct((batch * seq, heads * vd), F32),
        scratch_shapes=[pltpu.VMEM((hps * tq, LANES), F32), pltpu.VMEM((hps * tq, LANES), F32),
                        pltpu.VMEM((hps * tq, LANES), F32)],
        compiler_params=_cparams(("parallel", "parallel", "arbitrary")),
        name="mla_prompt",
    )(q, k, v)


def _sb_kernel(q_ref, k_ref, v_ref, tri_ref, o_ref, acc_ref, car_ref, pend_ref, z_ref, a_ref, *, tb, hd, npair):
    i = pl.program_id(2)
    lane = lax.broadcasted_iota(jnp.int32, (tb, LANES), 1)
    qs = []
    for p in range(npair):
        q2 = q_ref[:, p * LANES:(p + 1) * LANES].astype(F32)
        qs.append(jnp.concatenate([jnp.where(lane < hd, q2, 0.0), jnp.where(lane >= hd, q2, 0.0)],
                                  axis=0).astype(BF16))
    acc_ref[...] = jnp.zeros(acc_ref.shape, F32)
    car_ref[...] = jnp.zeros(car_ref.shape, F32)
    pend_ref[...] = jnp.zeros(pend_ref.shape, F32)

    def scores(j):
        off = pl.multiple_of(j * tb, tb)
        return [lax.dot_general(qs[p], k_ref[pl.ds(off, tb), p * LANES:(p + 1) * LANES], _NT,
                                preferred_element_type=F32) for p in range(npair)]

    def flush(j):
        off = pl.multiple_of(j * tb, tb)
        acc_ref[...] += jnp.concatenate(
            [jnp.dot(a_ref[p * 2 * tb:(p + 1) * 2 * tb], v_ref[pl.ds(off, tb), p * LANES:(p + 1) * LANES],
                     preferred_element_type=F32) for p in range(npair)], axis=0)

    def step(j, masked, zs):
        tri = tri_ref[...]
        if masked:
            valid = [(lax.broadcasted_iota(jnp.int32, (2 * tb, LANES), 1) + c * LANES)
                     < (lax.broadcasted_iota(jnp.int32, (2 * tb, LANES), 0) & (tb - 1))
                     for c in range(tb // LANES)]
        pairs = range(npair)
        carry = car_ref[...] - jnp.sum(pend_ref[...], axis=1, keepdims=True)
        sps = [_softplus2(z) for z in zs]
        if masked:
            sps = [jnp.concatenate([jnp.where(valid[c], sp[:, c * LANES:(c + 1) * LANES], 0.0)
                                    for c in range(tb // LANES)], axis=1) for sp in sps]
        incls = [jnp.dot(sp.astype(BF16), tri, preferred_element_type=F32) for sp in sps]
        weights = []
        for p in pairs:
            cp = carry[p * 2 * tb:(p + 1) * 2 * tb]
            parts = []
            for c in range(tb // LANES):
                sl = slice(c * LANES, (c + 1) * LANES)
                a = jnp.exp2(zs[p][:, sl] + incls[p][:, sl] + cp)
                if masked:
                    a = jnp.where(valid[c], a, 0.0)
                parts.append(a)
            a = jnp.concatenate(parts, axis=1).astype(BF16)
            weights.append(a)
        a_ref[...] = jnp.concatenate(weights, axis=0)
        car_ref[...] = carry
        pend_ref[...] = jnp.concatenate(
            [functools.reduce(lambda x, y: x + y, [sp[:, c * LANES:(c + 1) * LANES] for c in range(tb // LANES)])
             for sp in sps], axis=0)

    z_diag = scores(i)
    z_ref[...] = jnp.concatenate(scores(jnp.maximum(i - 1, 0)), axis=0)
    step(i, True, z_diag)

    def body(t, carry):
        j = i - 1 - t
        z_all = z_ref[...]
        z_next = jnp.concatenate(scores(jnp.maximum(j - 1, 0)), axis=0)
        flush(j + 1)
        step(j, False, [z_all[p * 2 * tb:(p + 1) * 2 * tb] for p in range(npair)])
        z_ref[...] = z_next
        return carry

    lax.fori_loop(0, i, body, 0)
    flush(0)
    for p in range(npair):
        o_ref[:, p * LANES:(p + 1) * LANES] = jnp.where(
            lane < hd, acc_ref[(2 * p) * tb:(2 * p + 1) * tb], acc_ref[(2 * p + 1) * tb:(2 * p + 2) * tb])


def _sb_prompt(sqb, skb, svb, tri, *, batch, seq, dims, tb, npair):
    hd = dims["sb_head_dim"]
    width = sqb.shape[-1]
    bw = npair * 2 * hd
    nq = seq // tb
    return pl.pallas_call(
        functools.partial(_sb_kernel, tb=tb, hd=hd, npair=npair),
        grid=(batch, width // bw, nq),
        in_specs=[pl.BlockSpec((tb, bw), lambda b, g, i: (b * nq + i, g)),
                  pl.BlockSpec((seq, bw), lambda b, g, i: (b, g)),
                  pl.BlockSpec((seq, bw), lambda b, g, i: (b, g)),
                  _const_spec(tri.shape)],
        out_specs=pl.BlockSpec((tb, bw), lambda b, g, i: (b * nq + i, g)),
        out_shape=jax.ShapeDtypeStruct((batch * seq, width), F32),
        scratch_shapes=[pltpu.VMEM((npair * 2 * tb, 2 * hd), F32), pltpu.VMEM((npair * 2 * tb, LANES), F32),
                        pltpu.VMEM((npair * 2 * tb, LANES), F32), pltpu.VMEM((npair * 2 * tb, tb), F32),
                        pltpu.VMEM((npair * 2 * tb, tb), BF16)],
        compiler_params=_cparams(("parallel", "parallel", "arbitrary")),
        name="sb_prompt",
    )(sqb, skb, svb, tri)


def _sample_kernel(pt_ref, qm_ref, qb_ref, ckvn_ref, krn_ref, skn_ref, svn_ref, tri_ref,
                   ckv_hbm, krt_hbm, kt_hbm, vt_hbm,
                   olat_ref, osb_ref,
                   ckv_buf, krt_buf, kt_buf, vt_buf, sem, m_ref, l_ref, accl_ref, accs_ref, car_ref,
                   *, layer, n_pages, ch, page, nslot, heads, kv_lora, rope, hd, dec_seq):
    b = pl.program_id(0)
    nb = pl.num_programs(0)
    nch = n_pages // ch
    ahead = nslot - 1
    rows = dec_seq * heads
    keys = ch * page
    nsub = keys // MXU_TILE

    def copies(bb, k, slot):
        c = nch - 1 - k
        out = []
        for j in range(ch):
            pg = pt_ref[bb * n_pages + c * ch + j]
            ks = pl.ds(j * page, page)
            out.append(pltpu.make_async_copy(ckv_hbm.at[layer, pg], ckv_buf.at[slot, ks, :], sem.at[slot, 0]))
            out.append(pltpu.make_async_copy(krt_hbm.at[layer, pg], krt_buf.at[slot, :, ks], sem.at[slot, 1]))
            out.append(pltpu.make_async_copy(kt_hbm.at[layer, pg], kt_buf.at[slot, :, ks], sem.at[slot, 2]))
            out.append(pltpu.make_async_copy(vt_hbm.at[layer, pg], vt_buf.at[slot, :, ks], sem.at[slot, 3]))
        return out

    def start(bb, k, slot):
        for cp in copies(bb, k, slot):
            cp.start()

    def wait(bb, k, slot):
        for cp in copies(bb, k, slot):
            cp.wait()

    @pl.when(b == 0)
    def _():
        for k in range(ahead):
            start(0, k, k)

    qm = qm_ref[0]
    qlat = qm[:, 0:kv_lora]
    qrope = qm[:, kv_lora:kv_lora + rope]
    qb = qb_ref[0]
    tok = lax.broadcasted_iota(jnp.int32, (rows, 1), 0) // heads

    qlat_f = qlat.astype(F32)
    qrope_f = qrope.astype(F32)
    qb_f = qb.astype(F32)
    ckvn = ckvn_ref[0]
    krn = krn_ref[0]
    skn = skn_ref[0]
    svn = svn_ref[0]
    m = jnp.full((rows, 1), NEG_INF, F32)
    l = jnp.zeros((rows, 1), F32)
    accl = jnp.zeros((rows, kv_lora), F32)
    for s in range(dec_seq):
        sc = (jnp.sum(qlat_f * ckvn[s:s + 1, :], axis=1, keepdims=True)
              + jnp.sum(qrope_f * krn[s:s + 1, :], axis=1, keepdims=True))
        sc = jnp.where(tok >= s, sc, NEG_INF)
        m_new = jnp.maximum(m, sc)
        alpha = jnp.exp2(m - m_new)
        p = jnp.exp2(sc - m_new)
        l = alpha * l + p
        accl = alpha * accl + p * ckvn[s:s + 1, :]
        m = m_new
    carry = jnp.zeros((rows, 1), F32)
    accs = jnp.zeros((rows, heads * hd), F32)
    for s in reversed(range(dec_seq)):
        z = jnp.sum(qb_f * skn[s:s + 1, :], axis=1, keepdims=True)
        sp = _softplus2(z)
        valid = tok > s
        a = jnp.where(valid, jnp.exp2(z - sp + carry), 0.0)
        accs = accs + a * svn[s:s + 1, :]
        carry = carry - jnp.where(valid, sp, 0.0)
    m_ref[...] = m
    l_ref[...] = l
    accl_ref[...] = accl
    accs_ref[...] = accs
    car_ref[...] = carry

    def compute(slot):
        kc = ckv_buf[slot].astype(BF16)
        krt = krt_buf[slot].astype(BF16)
        s = (lax.dot_general(qlat, kc, _NT, preferred_element_type=F32)
             + jnp.dot(qrope, krt, preferred_element_type=F32))
        m_prev = m_ref[...]
        m_new = jnp.maximum(m_prev, jnp.max(s, axis=1, keepdims=True))
        alpha = jnp.exp2(m_prev - m_new)
        p = jnp.exp2(s - m_new)
        l_ref[...] = alpha * l_ref[...] + jnp.sum(p, axis=1, keepdims=True)
        m_ref[...] = m_new
        accl_ref[...] = alpha * accl_ref[...] + jnp.dot(p.astype(BF16), kc, preferred_element_type=F32)

        kt = kt_buf[slot].astype(BF16)
        vt = vt_buf[slot].astype(BF16)
        z = jnp.dot(qb, kt, preferred_element_type=F32)
        sp = _softplus2(z)
        hi, lo = _split_bf16(sp)
        stacked = jnp.concatenate(
            [jnp.concatenate([hi[:, u * MXU_TILE:(u + 1) * MXU_TILE], lo[:, u * MXU_TILE:(u + 1) * MXU_TILE]],
                             axis=1) for u in range(nsub)], axis=0)
        incl = jnp.dot(stacked, tri_ref[...], preferred_element_type=F32)
        car = car_ref[...]
        parts = [None] * nsub
        for u in reversed(range(nsub)):
            sl = slice(u * MXU_TILE, (u + 1) * MXU_TILE)
            parts[u] = jnp.exp2(z[:, sl] + incl[u * rows:(u + 1) * rows] + car)
            car = car - jnp.sum(sp[:, sl], axis=1, keepdims=True)
        car_ref[...] = car
        a = jnp.concatenate(parts, axis=1).astype(BF16)
        accs_ref[...] += lax.dot_general(a, vt, _NT, preferred_element_type=F32)

    def group(it, carry_):
        for u in range(nslot):
            k = it * nslot + u
            kn = k + ahead
            bn = b + kn // nch

            @pl.when(bn < nb)
            def _():
                start(bn, kn % nch, (u + ahead) % nslot)

            wait(b, k, u)
            compute(u)
        return carry_

    lax.fori_loop(0, nch // nslot, group, 0)

    olat_ref[0] = accl_ref[...] / l_ref[...]
    accs = accs_ref[...]
    rid = lax.broadcasted_iota(jnp.int32, accs.shape, 0) % heads
    hid = lax.broadcasted_iota(jnp.int32, accs.shape, 1) // hd
    diag = jnp.where(rid == hid, accs, 0.0)
    osb_ref[0] = jnp.sum(diag.reshape(dec_seq, heads, heads * hd), axis=1)


def _sample_attn(pt_flat, qm, qb, ckvn, krn, skn, svn, tri, ckv_c, krt_c, kt_c, vt_c, *, layer, dims, ch, nslot):
    db, rows, qk = qm.shape
    heads, kv_lora, rope, hd = dims["heads"], dims["kv_lora"], dims["rope"], dims["sb_head_dim"]
    dec_seq = rows // heads
    page = ckv_c.shape[2]
    n_pages = pt_flat.shape[0] // db
    assert n_pages % (ch * nslot) == 0
    keys = ch * page
    width = heads * hd

    def per_b(shape):
        nd = len(shape)
        return pl.BlockSpec((1,) + tuple(shape[1:]), lambda b, pt: (b,) + (0,) * (nd - 1))

    anyspec = pl.BlockSpec(memory_space=pl.ANY)
    grid_spec = pltpu.PrefetchScalarGridSpec(
        num_scalar_prefetch=1,
        grid=(db,),
        in_specs=[per_b(qm.shape), per_b(qb.shape), per_b(ckvn.shape), per_b(krn.shape), per_b(skn.shape),
                  per_b(svn.shape), pl.BlockSpec(tri.shape, lambda b, pt: (0, 0)),
                  anyspec, anyspec, anyspec, anyspec],
        out_specs=[pl.BlockSpec((1, rows, kv_lora), lambda b, pt: (b, 0, 0)),
                   pl.BlockSpec((1, dec_seq, width), lambda b, pt: (b, 0, 0))],
        scratch_shapes=[pltpu.VMEM((nslot, keys, kv_lora), F32), pltpu.VMEM((nslot, rope, keys), F32),
                        pltpu.VMEM((nslot, width, keys), F32), pltpu.VMEM((nslot, width, keys), F32),
                        pltpu.SemaphoreType.DMA((nslot, 4)),
                        pltpu.VMEM((rows, 1), F32), pltpu.VMEM((rows, 1), F32),
                        pltpu.VMEM((rows, kv_lora), F32), pltpu.VMEM((rows, width), F32),
                        pltpu.VMEM((rows, 1), F32)],
    )
    return pl.pallas_call(
        functools.partial(_sample_kernel, layer=layer, n_pages=n_pages, ch=ch, page=page, nslot=nslot,
                          heads=heads, kv_lora=kv_lora, rope=rope, hd=hd, dec_seq=dec_seq),
        grid_spec=grid_spec,
        out_shape=[jax.ShapeDtypeStruct((db, rows, kv_lora), F32),
                   jax.ShapeDtypeStruct((db, dec_seq, width), F32)],
        compiler_params=_cparams(("arbitrary",)),
        name="sample_attn",
    )(pt_flat, qm, qb, ckvn, krn, skn, svn, tri, ckv_c, krt_c, kt_c, vt_c)


def _merge_kernel(x_ref, va_ref, osb_ref, sga_ref, sgb_ref, sma_ref, smb_ref, p_ref, wuv_ref, wa_ref, wb_ref,
                  wo_ref, postg_ref, pleg_ref, wpg_ref, wple_ref, y_ref, *, latent_in):
    va = va_ref[...]
    if latent_in:
        va = jnp.dot(va.astype(BF16), wuv_ref[...], preferred_element_type=F32)
    a = (va * sga_ref[...].astype(F32)).astype(BF16)
    bb = (osb_ref[...] * sgb_ref[...].astype(F32)).astype(BF16)
    ya = jnp.dot(a, wa_ref[...], preferred_element_type=F32)
    yb = jnp.dot(bb, wb_ref[...], preferred_element_type=F32)
    mix = (sma_ref[...].astype(F32) * ya + smb_ref[...].astype(F32) * yb).astype(BF16)
    mixed = jnp.dot(mix, wo_ref[...], preferred_element_type=F32)
    h = x_ref[...] + _rms(mixed, postg_ref[...])
    gate = _sigmoid(jnp.dot(_rms(h, pleg_ref[...]).astype(BF16), wpg_ref[...], preferred_element_type=F32))
    emb = jnp.dot(p_ref[...].astype(BF16), wple_ref[...], preferred_element_type=F32)
    y_ref[...] = h + gate * emb


def _merge(x2d, va, osb, sga, sgb, sma, smb, p2d, wuv_bd, wa, wb, wo, post_g, ple_g, wpg, wple, *,
           latent_in, tm):
    rows, d = x2d.shape

    def row(arr):
        return pl.BlockSpec((tm, arr.shape[-1]), lambda i: (i, 0))

    ins = [x2d, va, osb, sga, sgb, sma, smb, p2d]
    consts = [wuv_bd, wa, wb, wo, post_g, ple_g, wpg, wple]
    return pl.pallas_call(
        functools.partial(_merge_kernel, latent_in=latent_in),
        grid=(rows // tm,),
        in_specs=[row(a) for a in ins] + [_const_spec(c.shape) for c in consts],
        out_specs=pl.BlockSpec((tm, d), lambda i: (i, 0)),
        out_shape=jax.ShapeDtypeStruct((rows, d), F32),
        compiler_params=_cparams(("parallel",)),
        name="merge",
    )(*ins, *consts)


def _rope_tables(pos, dr, lead):
    freqs = ROPE_BASE ** (-jnp.arange(0, dr, 2, dtype=F32) / dr)
    ang = pos.astype(F32)[:, None] * freqs[None, :]
    c, s = jnp.cos(ang), jnp.sin(ang)
    n = pos.shape[0]
    tail = jnp.zeros((n, LANES - lead - dr), F32)
    return (jnp.concatenate([jnp.ones((n, lead), F32), c, c, tail], axis=1),
            jnp.concatenate([jnp.zeros((n, lead), F32), -s, s, tail], axis=1))


def _swap_halves(w):
    half = w.shape[-1] // 2
    return jnp.concatenate([w[..., half:], w[..., :half]], axis=-1)


def _place_lanes(w, lead):
    return jnp.pad(w, [(0, 0)] * (w.ndim - 1) + [(lead, LANES - lead - w.shape[-1])])


def kernel(x_prompt, x_sample, cache_mla_ckv, cache_mla_krope, cache_sb_k, cache_sb_v, page_table,
           p_prompt, p_sample, pre_norm_g, w_in, q_norm_g, w_uq, kv_norm_g, w_uk, w_uv,
           w_branch_a, w_branch_b, w_out, post_norm_g, ple_norm_g, w_ple_gate, w_ple):
    batch, seq, d = x_prompt.shape
    db, dec_seq, _ = x_sample.shape
    depth = w_in.shape[0]
    q_lora = q_norm_g.shape[1]
    kv_lora, heads, nope = w_uk.shape[1:]
    vdim = w_uv.shape[3]
    rope = cache_mla_krope.shape[3]
    n_phys, page, sb_heads, hd = cache_sb_k.shape[1:]
    n_pages = page_table.shape[1]
    past_len = n_pages * page
    mla_width = heads * vdim
    sb_width = sb_heads * hd
    assert sb_heads == heads and mla_width == sb_width and 2 * hd == LANES and nope + rope <= LANES
    q_scale = LOG2E / math.sqrt(nope + rope)
    sb_scale = LOG2E / math.sqrt(hd)
    dims = dict(heads=heads, kv_lora=kv_lora, rope=rope, mla_width=mla_width, sb_width=sb_width,
                sb_head_dim=hd)

    sizes = (q_lora, kv_lora, rope, mla_width, sb_width, sb_width, sb_width, sb_width, d, d)
    offs = [0]
    for n in sizes:
        offs.append(offs[-1] + n)
    names = ("cq", "ckv", "kra", "krb", "krah", "krbh", "ga", "sq", "sk", "sv", "gb", "ma", "mb")
    widths = (q_lora, kv_lora, LANES, LANES, LANES, LANES, mla_width, sb_width, sb_width, sb_width, sb_width, d, d)
    cols, lo = {}, 0
    for n, w in zip(names, widths):
        cols[n] = (lo, lo + w)
        lo += w

    pos_p = jnp.arange(seq, dtype=jnp.int32)
    pos_s = past_len + (jnp.arange(db * dec_seq, dtype=jnp.int32) % dec_seq)
    tabs_p = _rope_tables(pos_p, rope, 0) + _rope_tables(pos_p, rope, nope)
    tabs_s = _rope_tables(pos_s, rope, 0) + _rope_tables(pos_s, rope, nope)
    tri1 = -(lax.broadcasted_iota(jnp.int32, (MXU_TILE, MXU_TILE), 0)
             >= lax.broadcasted_iota(jnp.int32, (MXU_TILE, MXU_TILE), 1)).astype(BF16)
    tri = jnp.concatenate([tri1, tri1], axis=0)

    krt_c = jnp.transpose(cache_mla_krope, (0, 1, 3, 2))
    kt_c = jnp.transpose(cache_sb_k, (0, 1, 3, 4, 2)).reshape(depth, n_phys, sb_width, page)
    vt_c = jnp.transpose(cache_sb_v, (0, 1, 3, 4, 2)).reshape(depth, n_phys, sb_width, page)
    pt_flat = page_table.reshape(-1)
    head_mask = (jnp.arange(heads)[:, None] == (jnp.arange(sb_width) // hd)[None, :])

    hp = x_prompt.reshape(batch * seq, d)
    hs = x_sample.reshape(db * dec_seq, d)
    outs = [[] for _ in range(8)]
    tm_p = 256
    for i in range(depth):
        w = w_in[i]
        parts = [w[:, offs[k]:offs[k + 1]] for k in range(len(sizes))]
        cq_w, ckv_w, kr_w, ga_w, sq_w, sk_w, sv_w, gb_w, ma_w, mb_w = parts
        kr_sw = _swap_halves(kr_w)
        w_in2 = jnp.concatenate(
            [cq_w, ckv_w, _place_lanes(kr_w, 0), _place_lanes(kr_sw, 0), _place_lanes(kr_w, nope),
             _place_lanes(kr_sw, nope), ga_w, sq_w * sb_scale, sk_w, sv_w, gb_w, ma_w, mb_w], axis=1).astype(BF16)
        uq = w_uq[i].reshape(q_lora, heads, nope + rope) * q_scale
        uq_nope, uq_rope = uq[:, :, :nope], uq[:, :, nope:]
        hw = heads * LANES
        wqr_s = _place_lanes(uq_rope, 0).reshape(q_lora, hw).astype(BF16)
        wqrs_s = _place_lanes(_swap_halves(uq_rope), 0).reshape(q_lora, hw).astype(BF16)
        w_qlat = _fold_qlat(jnp.transpose(w_uq[i].reshape(q_lora, heads, nope + rope)[:, :, :nope], (1, 0, 2)),
                            jnp.transpose(w_uk[i], (1, 0, 2)), q_scale)
        wq_p = _place_lanes(jnp.concatenate([uq_nope, uq_rope], axis=-1), 0).reshape(q_lora, hw).astype(BF16)
        wqs_p = _place_lanes(_swap_halves(uq_rope), nope).reshape(q_lora, hw).astype(BF16)
        wk_p = _place_lanes(w_uk[i], 0).reshape(kv_lora, hw).astype(BF16)
        wv_p = w_uv[i].reshape(kv_lora, mla_width).astype(BF16)
        uv = jnp.transpose(w_uv[i], (1, 0, 2))
        wuv_bd = (uv[:, :, None, :] * jnp.eye(heads, dtype=F32)[:, None, :, None]).reshape(
            heads * kv_lora, mla_width).astype(BF16)
        wa, wb, wo = w_branch_a[i].astype(BF16), w_branch_b[i].astype(BF16), w_out[i].astype(BF16)
        wpg, wple = w_ple_gate[i].astype(BF16), w_ple[i].astype(BF16)
        pre_g, q_g, kv_g = pre_norm_g[i][None], q_norm_g[i][None], kv_norm_g[i][None]
        post_g, ple_g = post_norm_g[i][None], ple_norm_g[i][None]
        wkvt = jnp.concatenate([sk_w, sv_w], axis=1).T.astype(BF16)
        proj = functools.partial(_proj, pre_g=pre_g, q_g=q_g, kv_g=kv_g, w_in2=w_in2, wkvt=wkvt, cols=cols,
                                 dims=dims)
        merge = functools.partial(_merge, wuv_bd=wuv_bd, wa=wa, wb=wb, wo=wo, post_g=post_g, ple_g=ple_g,
                                  wpg=wpg, wple=wple)

        (ckv, krope, sga, sqb, sk, sv, sgb, sma, smb, q, k, v, skb, svb) = proj(
            hp, tabs_p, wq=wq_p, wqs=wqs_p, wk=wk_p, wv=wv_p, tm=tm_p, absorbed=False, seq=seq)
        va = _mla_prompt(q, k, v, batch=batch, seq=seq, dims=dims, hps=4, tq=256, tk=512)
        osb = _sb_prompt(sqb, skb, svb, tri1, batch=batch, seq=seq, dims=dims, tb=MXU_TILE, npair=2)
        hp = merge(hp, va, osb, sga, sgb, sma, smb, p_prompt[i].reshape(batch * seq, -1),
                   latent_in=False, tm=tm_p)
        outs[0].append(ckv.reshape(batch, seq, kv_lora))
        outs[1].append(krope.reshape(batch, seq, rope))
        outs[2].append(jnp.transpose(sk.reshape(batch, sb_heads, hd, seq), (0, 3, 1, 2)))
        outs[3].append(jnp.transpose(sv.reshape(batch, sb_heads, hd, seq), (0, 3, 1, 2)))

        rows_s = db * dec_seq
        (ckv, krope, sga, sqb, sk, sv, sgb, sma, smb, q) = proj(
            hs, tabs_s, wq=wqr_s, wqs=wqrs_s, wk=w_qlat, wv=wv_p, tm=rows_s, absorbed=True)
        qm = jnp.transpose(q.reshape(heads, db, dec_seq, -1), (1, 2, 0, 3)).reshape(db, dec_seq * heads, -1)
        qb = jnp.where(head_mask[None, None], sqb.reshape(db, dec_seq, 1, sb_width),
                       jnp.zeros((), BF16)).reshape(db, dec_seq * heads, sb_width)
        olat, osb = _sample_attn(
            pt_flat, qm, qb, ckv.reshape(db, dec_seq, kv_lora), krope.reshape(db, dec_seq, rope),
            sk.reshape(db, dec_seq, sb_width), sv.reshape(db, dec_seq, sb_width), tri,
            cache_mla_ckv, krt_c, kt_c, vt_c, layer=i, dims=dims, ch=8, nslot=4)
        hs = merge(hs, olat.reshape(rows_s, heads * kv_lora), osb.reshape(rows_s, sb_width), sga, sgb, sma, smb,
                   p_sample[i].reshape(rows_s, -1), latent_in=True, tm=rows_s)
        outs[4].append(ckv.reshape(db, dec_seq, kv_lora))
        outs[5].append(krope.reshape(db, dec_seq, rope))
        outs[6].append(sk.reshape(db, dec_seq, sb_heads, hd))
        outs[7].append(sv.reshape(db, dec_seq, sb_heads, hd))

    return (hp.reshape(batch, seq, d), hs.reshape(db, dec_seq, d)) + tuple(jnp.stack(o, axis=0) for o in outs)
```
